```python
import jax, jax.numpy as jnp
from jax import lax
import numpy as np

D_MODEL = 2048
BATCH = 16
SEQ = 256
DEPTH = 4
DEC_BATCH = 8
DEC_SEQ = 4096
PAST_LEN = 512

GRID_W = 64
D_MIX = D_MODEL
D_CONV = D_MIX // 2
D_RNN = D_MIX - D_CONV
N_RNN_HEADS = 16
RNN_HEAD_DIM = D_RNN // N_RNN_HEADS
CONV_A_W = 3
RNN_CONV_W = 4
FFN_CONV_W = 3
D_FF = 3 * D_MODEL
RGLRU_C = 8.0
D_IN = 3 * D_CONV + 2 * D_RNN
N_MOD = 6
EPS = 1e-6

kernel_name = "hybrid_shortconv_rglru_diffusion_step"


def rmsnorm(x, g):
    xf = x.astype(jnp.float32)
    var = jnp.mean(xf * xf, axis=-1, keepdims=True)
    return (xf * lax.rsqrt(var + EPS) * g.astype(jnp.float32)).astype(x.dtype)


def centred_dwconv(x, w):
    k = w.shape[0]
    p = k // 2
    length = x.shape[1]
    xp = jnp.pad(x, ((0, 0), (p, p), (0, 0)))
    return sum(xp[:, i:i + length] * w[i] for i in range(k))


def directional_dwconv(x, w, b):
    k = w.shape[0]
    length = x.shape[1]
    xp = jnp.pad(x, ((0, 0), (k - 1, 0), (0, 0)))
    return sum(xp[:, i:i + length] * w[i] for i in range(k)) + b


def _linear_combine(left, right):
    a1, b1 = left
    a2, b2 = right
    return a1 * a2, a2 * b1 + b2


def rglru_direction(x, conv_w, conv_b, w_a, b_a, w_x, b_x, lam, h0):
    bsz, length, _ = x.shape
    xc = directional_dwconv(x, conv_w, conv_b)
    xh = xc.reshape(bsz, length, N_RNN_HEADS, RNN_HEAD_DIM)
    r = jax.nn.sigmoid(jnp.einsum("blhi,hij->blhj", xh, w_a).reshape(bsz, length, D_RNN) + b_a)
    i = jax.nn.sigmoid(jnp.einsum("blhi,hij->blhj", xh, w_x).reshape(bsz, length, D_RNN) + b_x)
    log_a = -RGLRU_C * r.astype(jnp.float32) * jax.nn.softplus(-lam.astype(jnp.float32))
    a = jnp.exp(log_a)
    v = jnp.sqrt(-jnp.expm1(2.0 * log_a)) * (i * xc).astype(jnp.float32)
    a_cum, h_from_zero = lax.associative_scan(_linear_combine, (a, v), axis=1)
    h = h_from_zero + a_cum * h0.astype(jnp.float32)[:, None, :]
    return h.astype(x.dtype), h[:, -1].astype(x.dtype)


def token_mixers(u, h0, w_in, conv_a_w, rnn_conv_w, rnn_conv_b, rnn_w_a, rnn_b_a,
                 rnn_w_x, rnn_b_x, rnn_lam, w_out):
    proj = u @ w_in
    bg, cg, xa, xr, gr = jnp.split(
        proj, [D_CONV, 2 * D_CONV, 3 * D_CONV, 3 * D_CONV + D_RNN], axis=-1)
    y_a = bg * centred_dwconv(cg * xa, conv_a_w)
    h_f, last_f = rglru_direction(xr, rnn_conv_w[0], rnn_conv_b[0], rnn_w_a[0], rnn_b_a[0],
                                  rnn_w_x[0], rnn_b_x[0], rnn_lam[0], h0[:, 0])
    h_b, last_b = rglru_direction(xr[:, ::-1], rnn_conv_w[1], rnn_conv_b[1], rnn_w_a[1], rnn_b_a[1],
                                  rnn_w_x[1], rnn_b_x[1], rnn_lam[1], h0[:, 1])
    y_b = (h_f + h_b[:, ::-1]) * jax.nn.gelu(gr)
    y = jnp.concatenate([y_a, y_b], axis=-1) @ w_out
    return y, jnp.stack([last_f, last_b], axis=1)


def conv_ffn(u, w_up, conv_w, w_down):
    h = centred_dwconv(u @ w_up, conv_w)
    g, v = jnp.split(h, 2, axis=-1)
    return (jax.nn.gelu(g) * v) @ w_down


def to_col_major(x, rows):
    b, length, d = x.shape
    return x.reshape(b, rows, GRID_W, d).transpose(0, 2, 1, 3).reshape(b, length, d)


def to_row_major(x, rows):
    b, length, d = x.shape
    return x.reshape(b, GRID_W, rows, d).transpose(0, 2, 1, 3).reshape(b, length, d)


def trunk_layer(x, mod, h0, col_rows, norm_g, w_in, conv_a_w, rnn_conv_w, rnn_conv_b,
                rnn_w_a, rnn_b_a, rnn_w_x, rnn_b_x, rnn_lam, w_out, ffn_up, ffn_conv_w, ffn_down):
    sh1, sc1, g1, sh2, sc2, g2 = jnp.split(mod, N_MOD, axis=-1)
    u = rmsnorm(x, norm_g[0]) * (1.0 + sc1) + sh1
    if col_rows is not None:
        u = to_col_major(u, col_rows)
    y, h_last = token_mixers(u, h0, w_in, conv_a_w, rnn_conv_w, rnn_conv_b, rnn_w_a, rnn_b_a,
                             rnn_w_x, rnn_b_x, rnn_lam, w_out)
    if col_rows is not None:
        y = to_row_major(y, col_rows)
    x = x + g1 * rmsnorm(y, norm_g[1])
    u = rmsnorm(x, norm_g[2]) * (1.0 + sc2) + sh2
    x = x + g2 * rmsnorm(conv_ffn(u, ffn_up, ffn_conv_w, ffn_down), norm_g[3])
    return x, h_last


def setup_inputs(seed: int = 0) -> dict:
    key = jax.random.key(seed)
    ks = jax.random.split(key, 24)
    f32 = jnp.float32
    nrm = lambda k, shape, s: jax.random.normal(k, shape, f32) * s
    a0 = jax.random.uniform(ks[16], (DEPTH, 2, D_RNN), f32, 0.9, 0.999)
    return {
        "x_prompt": nrm(ks[0], (BATCH, SEQ, D_MODEL), 1.0),
        "x_sample": nrm(ks[1], (DEC_BATCH, DEC_SEQ, D_MODEL), 1.0),
        "state_h": nrm(ks[2], (DEC_BATCH, DEPTH, 2, D_RNN), 0.5),
        "c": nrm(ks[3], (DEC_BATCH, D_MODEL), 1.0),
        "c_ctx": nrm(ks[4], (D_MODEL,), 1.0),
        "w_ada": nrm(ks[5], (DEPTH, D_MODEL, N_MOD * D_MODEL), 0.5 * D_MODEL ** -0.5),
        "b_ada": nrm(ks[6], (DEPTH, N_MOD * D_MODEL), 0.01),
        "norm_g": 1.0 + nrm(ks[7], (DEPTH, 4, D_MODEL), 0.05),
        "w_in": nrm(ks[8], (DEPTH, D_MODEL, D_IN), D_MODEL ** -0.5),
        "conv_a_w": nrm(ks[9], (DEPTH, CONV_A_W, D_CONV), CONV_A_W ** -0.5),
        "rnn_conv_w": nrm(ks[10], (DEPTH, 2, RNN_CONV_W, D_RNN), RNN_CONV_W ** -0.5),
        "rnn_conv_b": nrm(ks[11], (DEPTH, 2, D_RNN), 0.01),
        "rnn_w_a": nrm(ks[12], (DEPTH, 2, N_RNN_HEADS, RNN_HEAD_DIM, RNN_HEAD_DIM), RNN_HEAD_DIM ** -0.5),
        "rnn_b_a": nrm(ks[13], (DEPTH, 2, D_RNN), 0.01),
        "rnn_w_x": nrm(ks[14], (DEPTH, 2, N_RNN_HEADS, RNN_HEAD_DIM, RNN_HEAD_DIM), RNN_HEAD_DIM ** -0.5),
        "rnn_b_x": nrm(ks[15], (DEPTH, 2, D_RNN), 0.01),
        "rnn_lam": jnp.log(a0) - jnp.log1p(-a0),
        "w_out": nrm(ks[17], (DEPTH, D_MIX, D_MODEL), D_MIX ** -0.5),
        "ffn_up": nrm(ks[18], (DEPTH, D_MODEL, 2 * D_FF), D_MODEL ** -0.5),
        "ffn_conv_w": nrm(ks[19], (DEPTH, FFN_CONV_W, 2 * D_FF), FFN_CONV_W ** -0.5),
        "ffn_down": nrm(ks[20], (DEPTH, D_FF, D_MODEL), D_FF ** -0.5),
    }


def reference(x_prompt, x_sample, state_h, c, c_ctx, w_ada, b_ada, norm_g, w_in, conv_a_w,
              rnn_conv_w, rnn_conv_b, rnn_w_a, rnn_b_a, rnn_w_x, rnn_b_x, rnn_lam, w_out,
              ffn_up, ffn_conv_w, ffn_down):
    rows = x_sample.shape[1] // GRID_W
    h_zero = jnp.zeros((x_prompt.shape[0], 2, D_RNN), x_prompt.dtype)
    xp, xs = x_prompt, x_sample
    ctx_states = []
    for l in range(DEPTH):
        lw = (norm_g[l], w_in[l], conv_a_w[l], rnn_conv_w[l], rnn_conv_b[l], rnn_w_a[l], rnn_b_a[l],
              rnn_w_x[l], rnn_b_x[l], rnn_lam[l], w_out[l], ffn_up[l], ffn_conv_w[l], ffn_down[l])
        mod_ctx = (jax.nn.silu(c_ctx) @ w_ada[l] + b_ada[l])[None, None, :]
        mod_lat = (jax.nn.silu(c) @ w_ada[l] + b_ada[l])[:, None, :]
        xp, h_ctx = trunk_layer(xp, mod_ctx, h_zero, None, *lw)
        ctx_states.append(h_ctx)
        xs, _ = trunk_layer(xs, mod_lat, state_h[:, l], rows if l % 2 == 1 else None, *lw)
    new_state_h = jnp.stack(ctx_states, axis=1)
    return (xp, xs, new_state_h)
```

```python
import functools

import jax
import jax.numpy as jnp
from jax import lax
from jax.experimental import pallas as pl
from jax.experimental.pallas import tpu as pltpu

GRID_W = 64
N_RNN_HEADS = 16
RGLRU_C = 8.0
N_MOD = 6
EPS = 1e-6
RNN_CONV_W = 4
CONV_A_W = 3
FFN_CONV_W = 3

SUBLANES = 8
LANES = 128
MXU_WIDTH = 256
VMEM_BYTES = 64 * 1024 * 1024
HALO_ROWS = 16
MOD_ROWS = 32

F32 = jnp.float32
BF16 = jnp.bfloat16


def _rms(x, g):
    var = jnp.mean(x * x, axis=-1, keepdims=True)
    return x * lax.rsqrt(var + EPS) * g


def _load_chunk(ref, c, ch):
    if len(ref.shape) == 2:
        return ref[pl.ds(pl.multiple_of(c * ch, ch), ch), :]
    r = ref.shape[1]
    t = ch // r
    return ref[pl.ds(pl.multiple_of(c * t, t), t), :, :].reshape(ch, ref.shape[2])


def _store_chunk(ref, c, ch, val):
    if len(ref.shape) == 2:
        ref[pl.ds(pl.multiple_of(c * ch, ch), ch), :] = val
    else:
        r = ref.shape[1]
        t = ch // r
        ref[pl.ds(pl.multiple_of(c * t, t), t), :, :] = val.reshape(t, r, ref.shape[2])


def _for_chunks(n, body):
    def wrapped(c, carry):
        body(c)
        return carry
    lax.fori_loop(0, n, wrapped, 0)


def _mod_body(cc_ref, w_ref, b_ref, o_ref):
    s = cc_ref[...]
    s = s * jax.nn.sigmoid(s)
    o_ref[...] = jnp.dot(s.astype(BF16), w_ref[...].astype(BF16),
                         preferred_element_type=F32) + b_ref[...]


def _modulation(cc, w_ada, b_ada, tn):
    depth, d, n = w_ada.shape
    rows = cc.shape[0]
    return pl.pallas_call(
        _mod_body,
        grid=(depth, n // tn),
        in_specs=[
            pl.BlockSpec((rows, d), lambda l, j: (0, 0)),
            pl.BlockSpec((None, d, tn), lambda l, j: (l, 0, j)),
            pl.BlockSpec((None, 1, tn), lambda l, j: (l, 0, j)),
        ],
        out_specs=pl.BlockSpec((None, rows, tn), lambda l, j: (l, 0, j)),
        out_shape=jax.ShapeDtypeStruct((depth, rows, n), F32),
        compiler_params=pltpu.CompilerParams(
            dimension_semantics=("arbitrary", "arbitrary"),
            vmem_limit_bytes=40 * 1024 * 1024),
        name="adaln_modulation",
    )(cc, w_ada, b_ada.reshape(depth, 1, n))


def _proj_body(x_ref, sc_ref, sh_ref, g_ref, w_ref, o_ref, u_ref, *, ch, tn):
    tm = u_ref.shape[0]

    def chunk(c):
        x = _load_chunk(x_ref, c, ch)
        u = _rms(x, g_ref[...]) * (1.0 + sc_ref[0:ch, :]) + sh_ref[0:ch, :]
        u_ref[pl.ds(pl.multiple_of(c * ch, ch), ch), :] = u.astype(BF16)

    _for_chunks(tm // ch, chunk)
    din = o_ref.shape[1]
    for lo in range(0, din, tn):
        hi = min(lo + tn, din)
        o_ref[:, lo:hi] = jnp.dot(u_ref[...], w_ref[:, lo:hi], preferred_element_type=F32).astype(o_ref.dtype)


def _x_spec(view, tile_rows):
    if view["kind"] == "flat":
        return pl.BlockSpec((tile_rows, view["d"]), lambda i: (i, 0))
    r = view["r"]
    steps = tile_rows // r
    tpc = view["grid_rows"] // steps
    return pl.BlockSpec((steps, None, r, view["d"]), lambda i: (i % tpc, i // tpc, 0, 0))


def _as_view(x, view):
    if view["kind"] == "flat":
        return x
    return x.reshape(view["grid_rows"], GRID_W, view["r"], view["d"])


def _proj(x, view, sc, sh, g, w, tm, ch, tn):
    n, d = x.shape
    din = w.shape[1]
    const = lambda i: (0, 0)
    return pl.pallas_call(
        functools.partial(_proj_body, ch=ch, tn=tn),
        grid=(n // tm,),
        in_specs=[
            _x_spec(view, tm),
            pl.BlockSpec(sc.shape, const),
            pl.BlockSpec(sh.shape, const),
            pl.BlockSpec(g.shape, const),
            pl.BlockSpec(w.shape, const, pipeline_mode=pl.Buffered(1)),
        ],
        out_specs=pl.BlockSpec((tm, din), lambda i: (i, 0)),
        out_shape=jax.ShapeDtypeStruct((n, din), F32),
        scratch_shapes=[pltpu.VMEM((tm, d), BF16)],
        compiler_params=pltpu.CompilerParams(
            dimension_semantics=("arbitrary",),
            vmem_limit_bytes=56 * 1024 * 1024),
        name="in_proj",
    )(_as_view(x, view), sc, sh, g, w)


def _fill_rnn_params(par_ref, ba_ref, bx_ref, lam_ref):
    dr = par_ref.shape[1]
    par_ref[0:8, :] = jnp.broadcast_to(ba_ref[...], (SUBLANES, dr))
    par_ref[8:16, :] = jnp.broadcast_to(bx_ref[...], (SUBLANES, dr))
    par_ref[16:24, :] = jnp.broadcast_to(-RGLRU_C * jax.nn.softplus(-lam_ref[...]), (SUBLANES, dr))


def _rnn_conv(xe_ref, xc_ref, cw_ref, cb_ref, m, r, ch, reverse):
    def chunk(c):
        r0 = pl.multiple_of(c * ch, ch)
        acc = jnp.broadcast_to(cb_ref[...], (ch, xc_ref.shape[1]))
        for k in range(RNN_CONV_W):
            off = (RNN_CONV_W - 1 - k) * r if reverse else k * r
            acc = acc + cw_ref[k:k + 1, :] * xe_ref[pl.ds(r0 + off, ch), :]
        xc_ref[pl.ds(r0, ch), :] = acc

    _for_chunks(m // ch, chunk)


def _rnn_gates(xc_ref, wg_ref, pre_ref):
    dr = xc_ref.shape[1]
    nblk, gw, _ = wg_ref.shape
    for k in range(nblk):
        p = jnp.dot(xc_ref[:, k * gw:(k + 1) * gw].astype(BF16), wg_ref[k],
                    preferred_element_type=F32)
        pre_ref[:, k * gw:(k + 1) * gw] = p[:, :gw]
        pre_ref[:, dr + k * gw:dr + (k + 1) * gw] = p[:, gw:]


def _scan_step(row, h, xc_ref, pre_ref, par_ref, nb, emit):
    dr = xc_ref.shape[1]
    ng = dr // LANES
    new_h = []
    for sb in range(nb):
        rs = pl.ds(row + sb * SUBLANES, SUBLANES)
        for g in range(ng):
            cs = slice(g * LANES, (g + 1) * LANES)
            gate_r = jax.nn.sigmoid(pre_ref[rs, cs] + par_ref[0:8, cs])
            gate_i = jax.nn.sigmoid(pre_ref[rs, dr + g * LANES:dr + (g + 1) * LANES] + par_ref[8:16, cs])
            xc = xc_ref[rs, cs]
            log_a = par_ref[16:24, cs] * gate_r
            a = jnp.exp(log_a)
            v = jnp.sqrt(1.0 - a * a) * (gate_i * xc)
            hh = a * h[sb * ng + g] + v
            emit(rs, cs, hh)
            new_h.append(hh)
    return tuple(new_h)


def _load_state(hc_ref, nb):
    ng = hc_ref.shape[1] // LANES
    return tuple(hc_ref[sb * SUBLANES:(sb + 1) * SUBLANES, g * LANES:(g + 1) * LANES]
                 for sb in range(nb) for g in range(ng))


def _store_state(hc_ref, h, nb):
    ng = hc_ref.shape[1] // LANES
    for sb in range(nb):
        for g in range(ng):
            hc_ref[sb * SUBLANES:(sb + 1) * SUBLANES, g * LANES:(g + 1) * LANES] = h[sb * ng + g]


def _bwd_scan_body(xr_ref, cw_ref, cb_ref, wg_ref, ba_ref, bx_ref, lam_ref, h0_ref,
                   hb_ref, hl_ref,
                   xe_ref, xc_ref, pre_ref, par_ref, hc_ref, *, r, ch):
    m, dr = xr_ref.shape
    nb = r // SUBLANES
    steps = m // r
    halo = (RNN_CONV_W - 1) * r

    @pl.when(pl.program_id(0) == 0)
    def _init():
        xe_ref[m:m + halo, :] = jnp.zeros((halo, dr), F32)
        hc_ref[...] = h0_ref[...]
        _fill_rnn_params(par_ref, ba_ref, bx_ref, lam_ref)

    def copy(c):
        r0 = pl.multiple_of(c * ch, ch)
        xe_ref[pl.ds(r0, ch), :] = xr_ref[pl.ds(r0, ch), :].astype(F32)

    _for_chunks(m // ch, copy)
    _rnn_conv(xe_ref, xc_ref, cw_ref, cb_ref, m, r, ch, reverse=True)
    xe_ref[m:m + halo, :] = xe_ref[0:halo, :]
    _rnn_gates(xc_ref, wg_ref, pre_ref)

    def emit(rs, cs, hh):
        hb_ref[rs, cs] = hh

    def step(tt, h):
        row = pl.multiple_of((steps - 1 - tt) * r, r)
        return _scan_step(row, h, xc_ref, pre_ref, par_ref, nb, emit)

    h = lax.fori_loop(0, steps, step, _load_state(hc_ref, nb))
    _store_state(hc_ref, h, nb)
    hl_ref[...] = hc_ref[...]


def _bwd_scan(proj, rp, h0, r, m, ch):
    n = proj.shape[0]
    dr = h0.shape[1]
    nt = n // m
    xr_col = rp["xr_col"]
    const2 = lambda i: (0, 0)
    const3 = lambda i: (0, 0, 0)
    halo = (RNN_CONV_W - 1) * r
    return pl.pallas_call(
        functools.partial(_bwd_scan_body, r=r, ch=ch),
        grid=(nt,),
        in_specs=[
            pl.BlockSpec((m, dr), lambda i: (nt - 1 - i, xr_col)),
            pl.BlockSpec(rp["cw"].shape, const2),
            pl.BlockSpec(rp["cb"].shape, const2),
            pl.BlockSpec(rp["wg"].shape, const3),
            pl.BlockSpec(rp["ba"].shape, const2),
            pl.BlockSpec(rp["bx"].shape, const2),
            pl.BlockSpec(rp["lam"].shape, const2),
            pl.BlockSpec(h0.shape, const2),
        ],
        out_specs=[
            pl.BlockSpec((m, dr), lambda i: (nt - 1 - i, 0)),
            pl.BlockSpec(h0.shape, const2),
        ],
        out_shape=[
            jax.ShapeDtypeStruct((n, dr), F32),
            jax.ShapeDtypeStruct(h0.shape, F32),
        ],
        scratch_shapes=[
            pltpu.VMEM((m + halo, dr), F32),
            pltpu.VMEM((m, dr), F32),
            pltpu.VMEM((m, 2 * dr), F32),
            pltpu.VMEM((3 * SUBLANES, dr), F32),
            pltpu.VMEM(h0.shape, F32),
        ],
        compiler_params=pltpu.CompilerParams(
            dimension_semantics=("arbitrary",),
            vmem_limit_bytes=40 * 1024 * 1024),
        name="rglru_bwd_scan",
    )(proj, rp["cw"], rp["cb"], rp["wg"], rp["ba"], rp["bx"], rp["lam"], h0)


def _mix_body(bg_ref, cg_ref, xa_ref, xr_ref, gr_ref, cgh_ref, xah_ref, hb_ref, x_ref,
              cw_ref, cb_ref, wg_ref, ba_ref, bx_ref, lam_ref, h0_ref,
              caw_ref, wout_ref, gn_ref, gate_ref,
              o_ref, hl_ref,
              xe_ref, xc_ref, pre_ref, par_ref, hc_ref, ze_ref, y_ref, t_ref, *, r, ch):
    m, dr = xr_ref.shape
    dc = bg_ref.shape[1]
    nb = r // SUBLANES
    steps = m // r
    halo = (RNN_CONV_W - 1) * r
    i = pl.program_id(0)
    last = pl.num_programs(0) - 1

    @pl.when(i == 0)
    def _init():
        xe_ref[0:halo, :] = jnp.zeros((halo, dr), F32)
        ze_ref[0:r, :] = jnp.zeros((r, dc), F32)
        hc_ref[...] = h0_ref[...]
        _fill_rnn_params(par_ref, ba_ref, bx_ref, lam_ref)

    def copy(c):
        r0 = pl.multiple_of(c * ch, ch)
        xe_ref[pl.ds(halo + r0, ch), :] = xr_ref[pl.ds(r0, ch), :].astype(F32)

    _for_chunks(m // ch, copy)
    _rnn_conv(xe_ref, xc_ref, cw_ref, cb_ref, m, r, ch, reverse=False)
    xe_ref[0:halo, :] = xe_ref[m:m + halo, :]
    _rnn_gates(xc_ref, wg_ref, pre_ref)

    def emit(rs, cs, hh):
        gate = jax.nn.gelu(gr_ref[rs, cs].astype(F32))
        y_ref[rs, dc + cs.start:dc + cs.stop] = (hh + hb_ref[rs, cs]) * gate

    def step(t, h):
        row = pl.multiple_of(t * r, r)
        return _scan_step(row, h, xc_ref, pre_ref, par_ref, nb, emit)

    h = lax.fori_loop(0, steps, step, _load_state(hc_ref, nb))
    _store_state(hc_ref, h, nb)
    hl_ref[...] = hc_ref[...]

    def zfill(c):
        r0 = pl.multiple_of(c * ch, ch)
        ze_ref[pl.ds(r + r0, ch), :] = (cg_ref[pl.ds(r0, ch), :].astype(F32)
                                        * xa_ref[pl.ds(r0, ch), :].astype(F32))

    _for_chunks(m // ch, zfill)
    z_next = cgh_ref[0:r, :].astype(F32) * xah_ref[0:r, :].astype(F32)
    ze_ref[r + m:2 * r + m, :] = jnp.where(i < last, z_next, 0.0)

    def conv_a(c):
        r0 = pl.multiple_of(c * ch, ch)
        acc = caw_ref[0:1, :] * ze_ref[pl.ds(r0, ch), :]
        acc = acc + caw_ref[1:2, :] * ze_ref[pl.ds(r0 + r, ch), :]
        acc = acc + caw_ref[2:3, :] * ze_ref[pl.ds(r0 + 2 * r, ch), :]
        y_ref[pl.ds(r0, ch), 0:dc] = bg_ref[pl.ds(r0, ch), :].astype(F32) * acc

    _for_chunks(m // ch, conv_a)
    ze_ref[0:r, :] = ze_ref[m:m + r, :]

    t_ref[...] = jnp.dot(y_ref[...].astype(BF16), wout_ref[...], preferred_element_type=F32)

    def finish(c):
        yo = t_ref[pl.ds(pl.multiple_of(c * ch, ch), ch), :]
        x = _load_chunk(x_ref, c, ch)
        _store_chunk(o_ref, c, ch, x + gate_ref[0:ch, :] * _rms(yo, gn_ref[...]))

    _for_chunks(m // ch, finish)


def _mix(proj, hb, x, view, rp, h0, caw, wout, gn, gate, r, m, ch, cols):
    n, d = x.shape
    dr = h0.shape[1]
    dc = caw.shape[1]
    dmix = wout.shape[0]
    nt = n // m
    hr = r
    nh = n // hr
    const2 = lambda i: (0, 0)
    const3 = lambda i: (0, 0, 0)
    halo = (RNN_CONV_W - 1) * r
    col = lambda k: pl.BlockSpec((m, dc), lambda i: (i, k))
    nxt = lambda k: pl.BlockSpec((hr, dc), lambda i: (jnp.minimum((i + 1) * (m // hr), nh - 1), k))
    xspec = _x_spec(view, m)
    return pl.pallas_call(
        functools.partial(_mix_body, r=r, ch=ch),
        grid=(nt,),
        in_specs=[
            col(cols["bg"]), col(cols["cg"]), col(cols["xa"]), col(cols["xr"]), col(cols["gr"]),
            nxt(cols["cg"]), nxt(cols["xa"]),
            pl.BlockSpec((m, dr), lambda i: (i, 0)),
            xspec,
            pl.BlockSpec(rp["cw"].shape, const2),
            pl.BlockSpec(rp["cb"].shape, const2),
            pl.BlockSpec(rp["wg"].shape, const3),
            pl.BlockSpec(rp["ba"].shape, const2),
            pl.BlockSpec(rp["bx"].shape, const2),
            pl.BlockSpec(rp["lam"].shape, const2),
            pl.BlockSpec(h0.shape, const2),
            pl.BlockSpec(caw.shape, const2),
            pl.BlockSpec(wout.shape, const2, pipeline_mode=pl.Buffered(1)),
            pl.BlockSpec(gn.shape, const2),
            pl.BlockSpec(gate.shape, const2),
        ],
        out_specs=[
            xspec,
            pl.BlockSpec(h0.shape, const2),
        ],
        out_shape=[
            jax.ShapeDtypeStruct(_as_view(x, view).shape, F32),
            jax.ShapeDtypeStruct(h0.shape, F32),
        ],
        scratch_shapes=[
            pltpu.VMEM((m + halo, dr), F32),
            pltpu.VMEM((m, dr), F32),
            pltpu.VMEM((m, 2 * dr), F32),
            pltpu.VMEM((3 * SUBLANES, dr), F32),
            pltpu.VMEM(h0.shape, F32),
            pltpu.VMEM((m + 2 * r, dc), F32),
            pltpu.VMEM((m, dmix), F32),
            pltpu.VMEM((m, d), F32),
        ],
        compiler_params=pltpu.CompilerParams(
            dimension_semantics=("arbitrary",),
            vmem_limit_bytes=56 * 1024 * 1024),
        name="mix_fwd",
    )(proj, proj, proj, proj, proj, proj, proj, hb, _as_view(x, view),
      rp["cw"], rp["cb"], rp["wg"], rp["ba"], rp["bx"], rp["lam"], h0, caw, wout, gn, gate)


def _ffn_body(x_ref, xp_ref, xn_ref, sc_ref, sh_ref, gate_ref, gin_ref, gout_ref,
              wup_ref, cw_ref, wd_ref, o_ref, u_ref, h_ref, a_ref, *, r, ch, ch2):
    tm, d = x_ref.shape
    tf = wd_ref.shape[0]
    i = pl.program_id(0)
    j = pl.program_id(1)
    ni = pl.num_programs(0)
    nj = pl.num_programs(1)
    hal = HALO_ROWS

    @pl.when(j == 0)
    def _norm():
        def chunk(c):
            x = _load_chunk(x_ref, c, ch)
            u = _rms(x, gin_ref[...]) * (1.0 + sc_ref[0:ch, :]) + sh_ref[0:ch, :]
            u_ref[pl.ds(pl.multiple_of(hal + c * ch, ch), ch), :] = u.astype(BF16)

        _for_chunks(tm // ch, chunk)
        for ref, lo in ((xp_ref, 0), (xn_ref, hal + tm)):
            u = _rms(ref[...], gin_ref[...]) * (1.0 + sc_ref[0:hal, :]) + sh_ref[0:hal, :]
            u_ref[lo:lo + hal, :] = u.astype(BF16)

    h_ref[...] = jnp.dot(u_ref[...], wup_ref[...], preferred_element_type=F32)
    h_ref[0:hal, :] = jnp.where(i > 0, h_ref[0:hal, :], 0.0)
    h_ref[hal + tm:2 * hal + tm, :] = jnp.where(i < ni - 1, h_ref[hal + tm:2 * hal + tm, :], 0.0)

    def act(c):
        r0 = pl.multiple_of(c * ch2, ch2)
        hc = cw_ref[0:1, :] * h_ref[pl.ds(r0 + hal - r, ch2), :]
        hc = hc + cw_ref[1:2, :] * h_ref[pl.ds(r0 + hal, ch2), :]
        hc = hc + cw_ref[2:3, :] * h_ref[pl.ds(r0 + hal + r, ch2), :]
        a_ref[pl.ds(r0, ch2), :] = (jax.nn.gelu(hc[:, :tf]) * hc[:, tf:]).astype(BF16)

    _for_chunks(tm // ch2, act)
    part = jnp.dot(a_ref[...], wd_ref[...], preferred_element_type=F32)

    @pl.when(j == 0)
    def _first():
        o_ref[...] = part

    @pl.when(j > 0)
    def _rest():
        o_ref[...] += part

    @pl.when(j == nj - 1)
    def _finish():
        def chunk(c):
            rows = pl.ds(pl.multiple_of(c * ch, ch), ch)
            o_ref[rows, :] = x_ref[rows, :] + gate_ref[0:ch, :] * _rms(o_ref[rows, :], gout_ref[...])

        _for_chunks(tm // ch, chunk)


def _ffn(x, sc, sh, gate, gin, gout, wup, cw, wd, r, tm, tf, ch, ch2):
    n, d = x.shape
    dff = wd.shape[0]
    ni = n // tm
    nj = dff // tf
    nh = n // HALO_ROWS
    const = lambda i, j: (0, 0)
    return pl.pallas_call(
        functools.partial(_ffn_body, r=r, ch=ch, ch2=ch2),
        grid=(ni, nj),
        in_specs=[
            pl.BlockSpec((tm, d), lambda i, j: (i, 0)),
            pl.BlockSpec((HALO_ROWS, d), lambda i, j: (jnp.maximum(i * (tm // HALO_ROWS) - 1, 0), 0)),
            pl.BlockSpec((HALO_ROWS, d), lambda i, j: (jnp.minimum((i + 1) * (tm // HALO_ROWS), nh - 1), 0)),
            pl.BlockSpec(sc.shape, const),
            pl.BlockSpec(sh.shape, const),
            pl.BlockSpec(gate.shape, const),
            pl.BlockSpec(gin.shape, const),
            pl.BlockSpec(gout.shape, const),
            pl.BlockSpec((d, 2 * tf), lambda i, j: (0, j)),
            pl.BlockSpec((FFN_CONV_W, 2 * tf), lambda i, j: (0, j)),
            pl.BlockSpec((tf, d), lambda i, j: (j, 0)),
        ],
        out_specs=pl.BlockSpec((tm, d), lambda i, j: (i, 0)),
        out_shape=jax.ShapeDtypeStruct((n, d), F32),
        scratch_shapes=[
            pltpu.VMEM((tm + 2 * HALO_ROWS, d), BF16),
            pltpu.VMEM((tm + 2 * HALO_ROWS, 2 * tf), F32),
            pltpu.VMEM((tm, tf), BF16),
        ],
        compiler_params=pltpu.CompilerParams(
            dimension_semantics=("arbitrary", "arbitrary"),
            vmem_limit_bytes=56 * 1024 * 1024),
        name="conv_ffn",
    )(x, x, x, sc, sh, gate, gin, gout, wup, cw, wd)


def _block_diag(w, gw):
    heads, hd, _ = w.shape
    hpb = gw // hd
    w = w.reshape(heads // hpb, hpb, hd, hd)
    eye = jnp.eye(hpb, dtype=w.dtype)
    return jnp.einsum("nhij,hg->nhigj", w, eye).reshape(heads // hpb, gw, gw)


def _interleave(w, tf):
    lead = w.shape[:-1]
    dff = w.shape[-1] // 2
    w = w.reshape(*lead, 2, dff // tf, tf)
    return jnp.swapaxes(w, -3, -2).reshape(*lead, 2 * dff)


def _plan(d, dr, dff, n_rows, r, grid_rows):
    seq_rows = grid_rows * r if grid_rows else n_rows
    return dict(
        proj_rows=min(512, seq_rows),
        mix_rows=min(256, seq_rows),
        ffn_rows=min(512, n_rows),
        ffn_cols=min(512, dff),
        wide_chunk=2 * SUBLANES,
        mix_chunk=4 * SUBLANES,
        ffn_chunk=2 * SUBLANES,
        proj_cols=min(1024, 3 * (d - dr) + 2 * dr),
    )


def _tile_rows(mod, r, rows):
    return jnp.tile(mod, (rows // r, 1))


def _layer(x, r, grid_rows, mods, h0, lw, plan):
    n, d = x.shape
    sh1, sc1, g1, sh2, sc2, g2 = [_tile_rows(m, r, MOD_ROWS) for m in jnp.split(mods, N_MOD, axis=-1)]
    if grid_rows:
        view = dict(kind="grid", r=r, d=d, grid_rows=grid_rows)
    else:
        view = dict(kind="flat", d=d)
    ng = lw["norm_g"]
    proj = _proj(x, view, sc1, sh1, ng[0:1], lw["w_in"], plan["proj_rows"], plan["wide_chunk"],
                 plan["proj_cols"])
    hb, last_b = _bwd_scan(proj, lw["rnn"][1], h0[:, 1], r, plan["mix_rows"], plan["mix_chunk"])
    x, last_f = _mix(proj, hb, x, view, lw["rnn"][0], h0[:, 0], lw["conv_a_w"], lw["w_out"],
                     ng[1:2], g1, r, plan["mix_rows"], plan["wide_chunk"], lw["cols"])
    x = x.reshape(n, d)
    x = _ffn(x, sc2, sh2, g2, ng[2:3], ng[3:4], lw["ffn_up"], lw["ffn_conv_w"], lw["ffn_down"],
             r, plan["ffn_rows"], plan["ffn_cols"], plan["wide_chunk"], plan["ffn_chunk"])
    return x, jnp.stack([last_f, last_b], axis=1)


def _forward(x_prompt, x_sample, state_h, c, c_ctx, w_ada, b_ada, norm_g, w_in, conv_a_w,
             rnn_conv_w, rnn_conv_b, rnn_w_a, rnn_b_a, rnn_w_x, rnn_b_x, rnn_lam, w_out,
             ffn_up, ffn_conv_w, ffn_down, plan_overrides=None):
    b, s, d = x_prompt.shape
    bd, l, _ = x_sample.shape
    depth = w_ada.shape[0]
    dr = rnn_lam.shape[-1]
    dc = conv_a_w.shape[-1]
    dff = ffn_down.shape[1]
    grid_rows = l // GRID_W
    gw = min(MXU_WIDTH, dr)

    plan_p = _plan(d, dr, dff, s * b, b, 0)
    plan_s = _plan(d, dr, dff, l * bd, bd, grid_rows)
    for p in (plan_p, plan_s):
        p.update(plan_overrides or {})

    xp = jnp.transpose(x_prompt, (1, 0, 2)).reshape(s * b, d)
    xs = jnp.transpose(x_sample, (1, 0, 2)).reshape(l * bd, d)

    cc = jnp.concatenate([c, jnp.broadcast_to(c_ctx[None, :], (SUBLANES, d))], axis=0)
    mods = _modulation(cc, w_ada, b_ada, min(1024, N_MOD * d))

    h_zero = jnp.zeros((b, 2, dr), F32)
    tf = plan_p["ffn_cols"]
    states = []
    for li in range(depth):
        rnn = []
        for di in range(2):
            wg = jnp.concatenate([_block_diag(rnn_w_a[li, di], gw), _block_diag(rnn_w_x[li, di], gw)],
                                 axis=-1).astype(BF16)
            rnn.append(dict(cw=rnn_conv_w[li, di], cb=rnn_conv_b[li, di][None, :], wg=wg,
                            ba=rnn_b_a[li, di][None, :], bx=rnn_b_x[li, di][None, :],
                            lam=rnn_lam[li, di][None, :], xr_col=3 * dc // dr))
        lw = dict(
            norm_g=norm_g[li],
            w_in=w_in[li].astype(BF16),
            conv_a_w=conv_a_w[li],
            rnn=rnn,
            w_out=w_out[li].astype(BF16),
            ffn_up=_interleave(ffn_up[li], tf).astype(BF16),
            ffn_conv_w=_interleave(ffn_conv_w[li], tf),
            ffn_down=ffn_down[li].astype(BF16),
            cols=dict(bg=0, cg=1, xa=2, xr=3 * dc // dr, gr=3 * dc // dr + 1),
        )
        mod_ctx = jnp.broadcast_to(mods[li, bd:bd + 1], (b, N_MOD * d))
        xp, st = _layer(xp, b, 0, mod_ctx, h_zero, lw, plan_p)
        states.append(st)
        xs, _ = _layer(xs, bd, grid_rows if li % 2 == 1 else 0, mods[li, :bd], state_h[:, li], lw, plan_s)

    y_prompt = jnp.transpose(xp.reshape(s, b, d), (1, 0, 2))
    y_sample = jnp.transpose(xs.reshape(l, bd, d), (1, 0, 2))
    return y_prompt, y_sample, jnp.stack(states, axis=1)


def kernel(x_prompt, x_sample, state_h, c, c_ctx, w_ada, b_ada, norm_g, w_in, conv_a_w, rnn_conv_w,
           rnn_conv_b, rnn_w_a, rnn_b_a, rnn_w_x, rnn_b_x, rnn_lam, w_out, ffn_up, ffn_conv_w, ffn_down):
    return _forward(x_prompt, x_sample, state_h, c, c_ctx, w_ada, b_ada, norm_g, w_in, conv_a_w,
                    rnn_conv_w, rnn_conv_b, rnn_w_a, rnn_b_a, rnn_w_x, rnn_b_x, rnn_lam, w_out,
                    ffn_up, ffn_conv_w, ffn_down)
```

```python
import functools

import jax
import jax.numpy as jnp
from jax import lax
from jax.experimental import pallas as pl
from jax.experimental.pallas import tpu as pltpu

GRID_W = 64
N_RNN_HEADS = 16
RGLRU_C = 8.0
N_MOD = 6
EPS = 1e-6
RNN_CONV_W = 4
CONV_A_W = 3
FFN_CONV_W = 3

SUBLANES = 8
LANES = 128
MXU_WIDTH = 256
VMEM_BYTES = 64 * 1024 * 1024
HALO_ROWS = 16
MOD_ROWS = 32
NORM_UNROLL = 4

F32 = jnp.float32
BF16 = jnp.bfloat16


def _rms(x, g):
    var = jnp.mean(x * x, axis=-1, keepdims=True)
    return x * lax.rsqrt(var + EPS) * g


def _sigmoid(x):
    return 0.5 * jnp.tanh(0.5 * x) + 0.5


def _load_chunk(ref, c, ch):
    if len(ref.shape) == 2:
        return ref[pl.ds(pl.multiple_of(c * ch, ch), ch), :]
    r = ref.shape[1]
    t = ch // r
    return ref[pl.ds(pl.multiple_of(c * t, t), t), :, :].reshape(ch, ref.shape[2])


def _store_chunk(ref, c, ch, val):
    if len(ref.shape) == 2:
        ref[pl.ds(pl.multiple_of(c * ch, ch), ch), :] = val
    else:
        r = ref.shape[1]
        t = ch // r
        ref[pl.ds(pl.multiple_of(c * t, t), t), :, :] = val.reshape(t, r, ref.shape[2])


def _for_chunks(n, body, unroll=1):
    def wrapped(c, carry):
        body(c)
        return carry
    lax.fori_loop(0, n, wrapped, 0, unroll=unroll)


def _mod_body(cc_ref, w_ref, b_ref, o_ref):
    s = cc_ref[...]
    s = s * jax.nn.sigmoid(s)
    o_ref[...] = jnp.dot(s.astype(BF16), w_ref[...].astype(BF16),
                         preferred_element_type=F32) + b_ref[...]


def _modulation(cc, w_ada, b_ada, tn):
    depth, d, n = w_ada.shape
    rows = cc.shape[0]
    return pl.pallas_call(
        _mod_body,
        grid=(depth, n // tn),
        in_specs=[
            pl.BlockSpec((rows, d), lambda l, j: (0, 0)),
            pl.BlockSpec((None, d, tn), lambda l, j: (l, 0, j)),
            pl.BlockSpec((None, 1, tn), lambda l, j: (l, 0, j)),
        ],
        out_specs=pl.BlockSpec((None, rows, tn), lambda l, j: (l, 0, j)),
        out_shape=jax.ShapeDtypeStruct((depth, rows, n), F32),
        compiler_params=pltpu.CompilerParams(
            dimension_semantics=("arbitrary", "arbitrary"),
            vmem_limit_bytes=40 * 1024 * 1024),
        name="adaln_modulation",
    )(cc, w_ada, b_ada.reshape(depth, 1, n))


def _proj_body(x_ref, sc_ref, sh_ref, g_ref, w_ref, o_ref, u_ref, *, ch, tn):
    tm = u_ref.shape[0]

    def chunk(c):
        x = _load_chunk(x_ref, c, ch)
        u = _rms(x, g_ref[...]) * (1.0 + sc_ref[0:ch, :]) + sh_ref[0:ch, :]
        u_ref[pl.ds(pl.multiple_of(c * ch, ch), ch), :] = u.astype(BF16)

    _for_chunks(tm // ch, chunk, unroll=NORM_UNROLL)
    din = o_ref.shape[1]
    for lo in range(0, din, tn):
        hi = min(lo + tn, din)
        o_ref[:, lo:hi] = jnp.dot(u_ref[...], w_ref[:, lo:hi], preferred_element_type=F32).astype(o_ref.dtype)


def _x_spec(view, tile_rows):
    if view["kind"] == "flat":
        return pl.BlockSpec((tile_rows, view["d"]), lambda i: (i, 0))
    r = view["r"]
    steps = tile_rows // r
    tpc = view["grid_rows"] // steps
    return pl.BlockSpec((steps, None, r, view["d"]), lambda i: (i % tpc, i // tpc, 0, 0))


def _as_view(x, view):
    if view["kind"] == "flat":
        return x
    return x.reshape(view["grid_rows"], GRID_W, view["r"], view["d"])


def _proj(x, view, sc, sh, g, w, tm, ch, tn):
    n, d = x.shape
    din = w.shape[1]
    const = lambda i: (0, 0)
    return pl.pallas_call(
        functools.partial(_proj_body, ch=ch, tn=tn),
        grid=(n // tm,),
        in_specs=[
            _x_spec(view, tm),
            pl.BlockSpec(sc.shape, const),
            pl.BlockSpec(sh.shape, const),
            pl.BlockSpec(g.shape, const),
            pl.BlockSpec(w.shape, const, pipeline_mode=pl.Buffered(1)),
        ],
        out_specs=pl.BlockSpec((tm, din), lambda i: (i, 0)),
        out_shape=jax.ShapeDtypeStruct((n, din), F32),
        scratch_shapes=[pltpu.VMEM((tm, d), BF16)],
        compiler_params=pltpu.CompilerParams(
            dimension_semantics=("arbitrary",),
            vmem_limit_bytes=56 * 1024 * 1024),
        name="in_proj",
    )(_as_view(x, view), sc, sh, g, w)


def _fill_rnn_params(par_ref, ba_ref, bx_ref, lam_ref):
    dr = par_ref.shape[1]
    par_ref[0:8, :] = jnp.broadcast_to(ba_ref[...], (SUBLANES, dr))
    par_ref[8:16, :] = jnp.broadcast_to(bx_ref[...], (SUBLANES, dr))
    par_ref[16:24, :] = jnp.broadcast_to(-RGLRU_C * jax.nn.softplus(-lam_ref[...]), (SUBLANES, dr))


def _rnn_conv(xe_ref, xc_ref, cw_ref, cb_ref, m, r, ch, reverse):
    def chunk(c):
        r0 = pl.multiple_of(c * ch, ch)
        acc = jnp.broadcast_to(cb_ref[...], (ch, xc_ref.shape[1]))
        for k in range(RNN_CONV_W):
            off = (RNN_CONV_W - 1 - k) * r if reverse else k * r
            acc = acc + cw_ref[k:k + 1, :] * xe_ref[pl.ds(r0 + off, ch), :]
        xc_ref[pl.ds(r0, ch), :] = acc

    _for_chunks(m // ch, chunk)


def _rnn_gates(xc_ref, wg_ref, pre_ref):
    dr = xc_ref.shape[1]
    nblk, gw, _ = wg_ref.shape
    for k in range(nblk):
        p = jnp.dot(xc_ref[:, k * gw:(k + 1) * gw].astype(BF16), wg_ref[k],
                    preferred_element_type=F32)
        pre_ref[:, k * gw:(k + 1) * gw] = p[:, :gw]
        pre_ref[:, dr + k * gw:dr + (k + 1) * gw] = p[:, gw:]


def _scan_step(row, h, xc_ref, pre_ref, par_ref, nb, emit):
    dr = xc_ref.shape[1]
    ng = dr // LANES
    new_h = []
    for sb in range(nb):
        rs = pl.ds(row + sb * SUBLANES, SUBLANES)
        for g in range(ng):
            cs = slice(g * LANES, (g + 1) * LANES)
            gate_r = _sigmoid(pre_ref[rs, cs] + par_ref[0:8, cs])
            gate_i = _sigmoid(pre_ref[rs, dr + g * LANES:dr + (g + 1) * LANES] + par_ref[8:16, cs])
            xc = xc_ref[rs, cs]
            log_a = par_ref[16:24, cs] * gate_r
            a = jnp.exp(log_a)
            v = jnp.sqrt(1.0 - a * a) * (gate_i * xc)
            hh = a * h[sb * ng + g] + v
            emit(rs, cs, hh)
            new_h.append(hh)
    return tuple(new_h)


def _load_state(hc_ref, nb):
    ng = hc_ref.shape[1] // LANES
    return tuple(hc_ref[sb * SUBLANES:(sb + 1) * SUBLANES, g * LANES:(g + 1) * LANES]
                 for sb in range(nb) for g in range(ng))


def _store_state(hc_ref, h, nb):
    ng = hc_ref.shape[1] // LANES
    for sb in range(nb):
        for g in range(ng):
            hc_ref[sb * SUBLANES:(sb + 1) * SUBLANES, g * LANES:(g + 1) * LANES] = h[sb * ng + g]


def _bwd_scan_body(xr_ref, cw_ref, cb_ref, wg_ref, ba_ref, bx_ref, lam_ref, h0_ref,
                   hb_ref, hl_ref,
                   xe_ref, xc_ref, pre_ref, par_ref, hc_ref, *, r, ch):
    m, dr = xr_ref.shape
    nb = r // SUBLANES
    steps = m // r
    halo = (RNN_CONV_W - 1) * r

    @pl.when(pl.program_id(0) == 0)
    def _init():
        xe_ref[m:m + halo, :] = jnp.zeros((halo, dr), F32)
        hc_ref[...] = h0_ref[...]
        _fill_rnn_params(par_ref, ba_ref, bx_ref, lam_ref)

    def copy(c):
        r0 = pl.multiple_of(c * ch, ch)
        xe_ref[pl.ds(r0, ch), :] = xr_ref[pl.ds(r0, ch), :].astype(F32)

    _for_chunks(m // ch, copy)
    _rnn_conv(xe_ref, xc_ref, cw_ref, cb_ref, m, r, ch, reverse=True)
    xe_ref[m:m + halo, :] = xe_ref[0:halo, :]
    _rnn_gates(xc_ref, wg_ref, pre_ref)

    def emit(rs, cs, hh):
        hb_ref[rs, cs] = hh

    def step(tt, h):
        row = pl.multiple_of((steps - 1 - tt) * r, r)
        return _scan_step(row, h, xc_ref, pre_ref, par_ref, nb, emit)

    h = lax.fori_loop(0, steps, step, _load_state(hc_ref, nb), unroll=2)
    _store_state(hc_ref, h, nb)
    hl_ref[...] = hc_ref[...]


def _bwd_scan(proj, rp, h0, r, m, ch):
    n = proj.shape[0]
    dr = h0.shape[1]
    nt = n // m
    xr_col = rp["xr_col"]
    const2 = lambda i: (0, 0)
    const3 = lambda i: (0, 0, 0)
    halo = (RNN_CONV_W - 1) * r
    return pl.pallas_call(
        functools.partial(_bwd_scan_body, r=r, ch=ch),
        grid=(nt,),
        in_specs=[
            pl.BlockSpec((m, dr), lambda i: (nt - 1 - i, xr_col)),
            pl.BlockSpec(rp["cw"].shape, const2),
            pl.BlockSpec(rp["cb"].shape, const2),
            pl.BlockSpec(rp["wg"].shape, const3),
            pl.BlockSpec(rp["ba"].shape, const2),
            pl.BlockSpec(rp["bx"].shape, const2),
            pl.BlockSpec(rp["lam"].shape, const2),
            pl.BlockSpec(h0.shape, const2),
        ],
        out_specs=[
            pl.BlockSpec((m, dr), lambda i: (nt - 1 - i, 0)),
            pl.BlockSpec(h0.shape, const2),
        ],
        out_shape=[
            jax.ShapeDtypeStruct((n, dr), F32),
            jax.ShapeDtypeStruct(h0.shape, F32),
        ],
        scratch_shapes=[
            pltpu.VMEM((m + halo, dr), F32),
            pltpu.VMEM((m, dr), F32),
            pltpu.VMEM((m, 2 * dr), F32),
            pltpu.VMEM((3 * SUBLANES, dr), F32),
            pltpu.VMEM(h0.shape, F32),
        ],
        compiler_params=pltpu.CompilerParams(
            dimension_semantics=("arbitrary",),
            vmem_limit_bytes=40 * 1024 * 1024),
        name="rglru_bwd_scan",
    )(proj, rp["cw"], rp["cb"], rp["wg"], rp["ba"], rp["bx"], rp["lam"], h0)


def _mix_body(bg_ref, cg_ref, xa_ref, xr_ref, gr_ref, cgh_ref, xah_ref, hb_ref, x_ref,
              cw_ref, cb_ref, wg_ref, ba_ref, bx_ref, lam_ref, h0_ref,
              caw_ref, wout_ref, gn_ref, gate_ref,
              o_ref, hl_ref,
              xe_ref, xc_ref, pre_ref, par_ref, hc_ref, ze_ref, y_ref, t_ref, *, r, ch):
    m, dr = xr_ref.shape
    dc = bg_ref.shape[1]
    nb = r // SUBLANES
    steps = m // r
    halo = (RNN_CONV_W - 1) * r
    i = pl.program_id(0)
    last = pl.num_programs(0) - 1

    @pl.when(i == 0)
    def _init():
        xe_ref[0:halo, :] = jnp.zeros((halo, dr), F32)
        ze_ref[0:r, :] = jnp.zeros((r, dc), F32)
        hc_ref[...] = h0_ref[...]
        _fill_rnn_params(par_ref, ba_ref, bx_ref, lam_ref)

    def copy(c):
        r0 = pl.multiple_of(c * ch, ch)
        xe_ref[pl.ds(halo + r0, ch), :] = xr_ref[pl.ds(r0, ch), :].astype(F32)

    _for_chunks(m // ch, copy)
    _rnn_conv(xe_ref, xc_ref, cw_ref, cb_ref, m, r, ch, reverse=False)
    xe_ref[0:halo, :] = xe_ref[m:m + halo, :]
    _rnn_gates(xc_ref, wg_ref, pre_ref)

    def emit(rs, cs, hh):
        gate = jax.nn.gelu(gr_ref[rs, cs].astype(F32))
        y_ref[rs, dc + cs.start:dc + cs.stop] = (hh + hb_ref[rs, cs]) * gate

    def step(t, h):
        row = pl.multiple_of(t * r, r)
        return _scan_step(row, h, xc_ref, pre_ref, par_ref, nb, emit)

    h = lax.fori_loop(0, steps, step, _load_state(hc_ref, nb), unroll=2)
    _store_state(hc_ref, h, nb)
    hl_ref[...] = hc_ref[...]

    def zfill(c):
        r0 = pl.multiple_of(c * ch, ch)
        ze_ref[pl.ds(r + r0, ch), :] = (cg_ref[pl.ds(r0, ch), :].astype(F32)
                                        * xa_ref[pl.ds(r0, ch), :].astype(F32))

    _for_chunks(m // ch, zfill)
    z_next = cgh_ref[0:r, :].astype(F32) * xah_ref[0:r, :].astype(F32)
    ze_ref[r + m:2 * r + m, :] = jnp.where(i < last, z_next, 0.0)

    def conv_a(c):
        r0 = pl.multiple_of(c * ch, ch)
        acc = caw_ref[0:1, :] * ze_ref[pl.ds(r0, ch), :]
        acc = acc + caw_ref[1:2, :] * ze_ref[pl.ds(r0 + r, ch), :]
        acc = acc + caw_ref[2:3, :] * ze_ref[pl.ds(r0 + 2 * r, ch), :]
        y_ref[pl.ds(r0, ch), 0:dc] = bg_ref[pl.ds(r0, ch), :].astype(F32) * acc

    _for_chunks(m // ch, conv_a)
    ze_ref[0:r, :] = ze_ref[m:m + r, :]

    t_ref[...] = jnp.dot(y_ref[...].astype(BF16), wout_ref[...], preferred_element_type=F32)

    def finish(c):
        yo = t_ref[pl.ds(pl.multiple_of(c * ch, ch), ch), :]
        x = _load_chunk(x_ref, c, ch)
        _store_chunk(o_ref, c, ch, x + gate_ref[0:ch, :] * _rms(yo, gn_ref[...]))

    _for_chunks(m // ch, finish, unroll=NORM_UNROLL)


def _mix(proj, hb, x, view, rp, h0, caw, wout, gn, gate, r, m, ch, cols):
    n, d = x.shape
    dr = h0.shape[1]
    dc = caw.shape[1]
    dmix = wout.shape[0]
    nt = n // m
    hr = r
    nh = n // hr
    const2 = lambda i: (0, 0)
    const3 = lambda i: (0, 0, 0)
    halo = (RNN_CONV_W - 1) * r
    col = lambda k: pl.BlockSpec((m, dc), lambda i: (i, k))
    nxt = lambda k: pl.BlockSpec((hr, dc), lambda i: (jnp.minimum((i + 1) * (m // hr), nh - 1), k))
    xspec = _x_spec(view, m)
    return pl.pallas_call(
        functools.partial(_mix_body, r=r, ch=ch),
        grid=(nt,),
        in_specs=[
            col(cols["bg"]), col(cols["cg"]), col(cols["xa"]), col(cols["xr"]), col(cols["gr"]),
            nxt(cols["cg"]), nxt(cols["xa"]),
            pl.BlockSpec((m, dr), lambda i: (i, 0)),
            xspec,
            pl.BlockSpec(rp["cw"].shape, const2),
            pl.BlockSpec(rp["cb"].shape, const2),
            pl.BlockSpec(rp["wg"].shape, const3),
            pl.BlockSpec(rp["ba"].shape, const2),
            pl.BlockSpec(rp["bx"].shape, const2),
            pl.BlockSpec(rp["lam"].shape, const2),
            pl.BlockSpec(h0.shape, const2),
            pl.BlockSpec(caw.shape, const2),
            pl.BlockSpec(wout.shape, const2, pipeline_mode=pl.Buffered(1)),
            pl.BlockSpec(gn.shape, const2),
            pl.BlockSpec(gate.shape, const2),
        ],
        out_specs=[
            xspec,
            pl.BlockSpec(h0.shape, const2),
        ],
        out_shape=[
            jax.ShapeDtypeStruct(_as_view(x, view).shape, F32),
            jax.ShapeDtypeStruct(h0.shape, F32),
        ],
        scratch_shapes=[
            pltpu.VMEM((m + halo, dr), F32),
            pltpu.VMEM((m, dr), F32),
            pltpu.VMEM((m, 2 * dr), F32),
            pltpu.VMEM((3 * SUBLANES, dr), F32),
            pltpu.VMEM(h0.shape, F32),
            pltpu.VMEM((m + 2 * r, dc), F32),
            pltpu.VMEM((m, dmix), F32),
            pltpu.VMEM((m, d), F32),
        ],
        compiler_params=pltpu.CompilerParams(
            dimension_semantics=("arbitrary",),
            vmem_limit_bytes=56 * 1024 * 1024),
        name="mix_fwd",
    )(proj, proj, proj, proj, proj, proj, proj, hb, _as_view(x, view),
      rp["cw"], rp["cb"], rp["wg"], rp["ba"], rp["bx"], rp["lam"], h0, caw, wout, gn, gate)


def _ffn_body(x_ref, xp_ref, xn_ref, sc_ref, sh_ref, gate_ref, gin_ref, gout_ref,
              wg_ref, wv_ref, cg_ref, cv_ref, wd_ref, o_ref, u_ref, acc_ref, *, r, ch, ts):
    tm, d = x_ref.shape
    tf = wd_ref.shape[0]
    i = pl.program_id(0)
    j = pl.program_id(1)
    ni = pl.num_programs(0)
    nj = pl.num_programs(1)
    hal = HALO_ROWS

    @pl.when(j == 0)
    def _norm():
        def chunk(c):
            x = _load_chunk(x_ref, c, ch)
            u = _rms(x, gin_ref[...]) * (1.0 + sc_ref[0:ch, :]) + sh_ref[0:ch, :]
            u_ref[pl.ds(pl.multiple_of(hal + c * ch, ch), ch), :] = u.astype(BF16)
            acc_ref[pl.ds(pl.multiple_of(c * ch, ch), ch), :] = jnp.zeros((ch, d), F32)

        _for_chunks(tm // ch, chunk, unroll=NORM_UNROLL)
        for ref, lo, inside in ((xp_ref, 0, i > 0), (xn_ref, hal + tm, i < ni - 1)):
            u = _rms(ref[...], gin_ref[...]) * (1.0 + sc_ref[0:hal, :]) + sh_ref[0:hal, :]
            u_ref[lo:lo + hal, :] = jnp.where(inside, u, 0.0).astype(BF16)

    u = u_ref[...]
    acc = None
    def hidden(w_ref, c_ref, cols):
        h = jnp.dot(u, w_ref[:, cols], preferred_element_type=F32)
        cw = c_ref[:, cols]
        return (cw[0:1, :] * h[hal - r:hal - r + tm, :] + cw[1:2, :] * h[hal:hal + tm, :]
                + cw[2:3, :] * h[hal + r:hal + r + tm, :])

    for k in range(tf // ts):
        cols = slice(ts * k, ts * (k + 1))
        a = (jax.nn.gelu(hidden(wg_ref, cg_ref, cols)) * hidden(wv_ref, cv_ref, cols)).astype(BF16)
        p = jnp.dot(a, wd_ref[cols, :], preferred_element_type=F32)
        acc = p if acc is None else acc + p
    acc_ref[...] += acc

    @pl.when(j == nj - 1)
    def _finish():
        def chunk(c):
            rows = pl.ds(pl.multiple_of(c * ch, ch), ch)
            o_ref[rows, :] = x_ref[rows, :] + gate_ref[0:ch, :] * _rms(acc_ref[rows, :], gout_ref[...])

        _for_chunks(tm // ch, chunk, unroll=NORM_UNROLL)


def _ffn(x, sc, sh, gate, gin, gout, wup, cw, wd, r, tm, tf, ts, ch):
    n, d = x.shape
    dff = wd.shape[0]
    ni = n // tm
    nj = dff // tf
    nh = n // HALO_ROWS
    const = lambda i, j: (0, 0)
    return pl.pallas_call(
        functools.partial(_ffn_body, r=r, ch=ch, ts=ts),
        grid=(ni, nj),
        in_specs=[
            pl.BlockSpec((tm, d), lambda i, j: (i, 0)),
            pl.BlockSpec((HALO_ROWS, d), lambda i, j: (jnp.maximum(i * (tm // HALO_ROWS) - 1, 0), 0)),
            pl.BlockSpec((HALO_ROWS, d), lambda i, j: (jnp.minimum((i + 1) * (tm // HALO_ROWS), nh - 1), 0)),
            pl.BlockSpec(sc.shape, const),
            pl.BlockSpec(sh.shape, const),
            pl.BlockSpec(gate.shape, const),
            pl.BlockSpec(gin.shape, const),
            pl.BlockSpec(gout.shape, const),
            pl.BlockSpec((d, tf), lambda i, j: (0, j)),
            pl.BlockSpec((d, tf), lambda i, j: (0, nj + j)),
            pl.BlockSpec((FFN_CONV_W, tf), lambda i, j: (0, j)),
            pl.BlockSpec((FFN_CONV_W, tf), lambda i, j: (0, nj + j)),
            pl.BlockSpec((tf, d), lambda i, j: (j, 0)),
        ],
        out_specs=pl.BlockSpec((tm, d), lambda i, j: (i, 0)),
        out_shape=jax.ShapeDtypeStruct((n, d), F32),
        scratch_shapes=[
            pltpu.VMEM((tm + 2 * HALO_ROWS, d), BF16),
            pltpu.VMEM((tm, d), F32),
        ],
        compiler_params=pltpu.CompilerParams(
            dimension_semantics=("arbitrary", "arbitrary"),
            vmem_limit_bytes=56 * 1024 * 1024),
        name="conv_ffn",
    )(x, x, x, sc, sh, gate, gin, gout, wup, wup, cw, cw, wd)


def _block_diag(w, gw):
    heads, hd, _ = w.shape
    hpb = gw // hd
    w = w.reshape(heads // hpb, hpb, hd, hd)
    eye = jnp.eye(hpb, dtype=w.dtype)
    return jnp.einsum("nhij,hg->nhigj", w, eye).reshape(heads // hpb, gw, gw)


def _plan(d, dr, dff, n_rows, r, grid_rows):
    seq_rows = grid_rows * r if grid_rows else n_rows
    return dict(
        proj_rows=min(512, seq_rows),
        mix_rows=min(256, seq_rows),
        ffn_rows=min(512, n_rows),
        ffn_cols=min(512, dff),
        wide_chunk=2 * SUBLANES,
        mix_chunk=4 * SUBLANES,
        ffn_sub=min(MXU_WIDTH, dff),
        proj_cols=min(1024, 3 * (d - dr) + 2 * dr),
    )


def _tile_rows(mod, r, rows):
    return jnp.tile(mod, (rows // r, 1))


def _layer(x, r, grid_rows, mods, h0, lw, plan):
    n, d = x.shape
    sh1, sc1, g1, sh2, sc2, g2 = [_tile_rows(m, r, MOD_ROWS) for m in jnp.split(mods, N_MOD, axis=-1)]
    if grid_rows:
        view = dict(kind="grid", r=r, d=d, grid_rows=grid_rows)
    else:
        view = dict(kind="flat", d=d)
    ng = lw["norm_g"]
    proj = _proj(x, view, sc1, sh1, ng[0:1], lw["w_in"], plan["proj_rows"], plan["wide_chunk"],
                 plan["proj_cols"])
    hb, last_b = _bwd_scan(proj, lw["rnn"][1], h0[:, 1], r, plan["mix_rows"], plan["mix_chunk"])
    x, last_f = _mix(proj, hb, x, view, lw["rnn"][0], h0[:, 0], lw["conv_a_w"], lw["w_out"],
                     ng[1:2], g1, r, plan["mix_rows"], plan["wide_chunk"], lw["cols"])
    x = x.reshape(n, d)
    x = _ffn(x, sc2, sh2, g2, ng[2:3], ng[3:4], lw["ffn_up"], lw["ffn_conv_w"], lw["ffn_down"],
             r, plan["ffn_rows"], plan["ffn_cols"], plan["ffn_sub"], plan["wide_chunk"])
    return x, jnp.stack([last_f, last_b], axis=1)


def _forward(x_prompt, x_sample, state_h, c, c_ctx, w_ada, b_ada, norm_g, w_in, conv_a_w,
             rnn_conv_w, rnn_conv_b, rnn_w_a, rnn_b_a, rnn_w_x, rnn_b_x, rnn_lam, w_out,
             ffn_up, ffn_conv_w, ffn_down, plan_overrides=None):
    b, s, d = x_prompt.shape
    bd, l, _ = x_sample.shape
    depth = w_ada.shape[0]
    dr = rnn_lam.shape[-1]
    dc = conv_a_w.shape[-1]
    dff = ffn_down.shape[1]
    grid_rows = l // GRID_W
    gw = min(MXU_WIDTH, dr)

    plan_p = _plan(d, dr, dff, s * b, b, 0)
    plan_s = _plan(d, dr, dff, l * bd, bd, grid_rows)
    for p in (plan_p, plan_s):
        p.update(plan_overrides or {})

    xp = jnp.transpose(x_prompt, (1, 0, 2)).reshape(s * b, d)
    xs = jnp.transpose(x_sample, (1, 0, 2)).reshape(l * bd, d)

    cc = jnp.concatenate([c, jnp.broadcast_to(c_ctx[None, :], (SUBLANES, d))], axis=0)
    mods = _modulation(cc, w_ada, b_ada, min(1024, N_MOD * d))

    h_zero = jnp.zeros((b, 2, dr), F32)
    states = []
    for li in range(depth):
        rnn = []
        for di in range(2):
            wg = jnp.concatenate([_block_diag(rnn_w_a[li, di], gw), _block_diag(rnn_w_x[li, di], gw)],
                                 axis=-1).astype(BF16)
            rnn.append(dict(cw=rnn_conv_w[li, di], cb=rnn_conv_b[li, di][None, :], wg=wg,
                            ba=rnn_b_a[li, di][None, :], bx=rnn_b_x[li, di][None, :],
                            lam=rnn_lam[li, di][None, :], xr_col=3 * dc // dr))
        lw = dict(
            norm_g=norm_g[li],
            w_in=w_in[li].astype(BF16),
            conv_a_w=conv_a_w[li],
            rnn=rnn,
            w_out=w_out[li].astype(BF16),
            ffn_up=ffn_up[li].astype(BF16),
            ffn_conv_w=ffn_conv_w[li],
            ffn_down=ffn_down[li].astype(BF16),
            cols=dict(bg=0, cg=1, xa=2, xr=3 * dc // dr, gr=3 * dc // dr + 1),
        )
        mod_ctx = jnp.broadcast_to(mods[li, bd:bd + 1], (b, N_MOD * d))
        xp, st = _layer(xp, b, 0, mod_ctx, h_zero, lw, plan_p)
        states.append(st)
        xs, _ = _layer(xs, bd, grid_rows if li % 2 == 1 else 0, mods[li, :bd], state_h[:, li], lw, plan_s)

    y_prompt = jnp.transpose(xp.reshape(s, b, d), (1, 0, 2))
    y_sample = jnp.transpose(xs.reshape(l, bd, d), (1, 0, 2))
    return y_prompt, y_sample, jnp.stack(states, axis=1)


def kernel(x_prompt, x_sample, state_h, c, c_ctx, w_ada, b_ada, norm_g, w_in, conv_a_w, rnn_conv_w,
           rnn_conv_b, rnn_w_a, rnn_b_a, rnn_w_x, rnn_b_x, rnn_lam, w_out, ffn_up, ffn_conv_w, ffn_down):
    return _forward(x_prompt, x_sample, state_h, c, c_ctx, w_ada, b_ada, norm_g, w_in, conv_a_w,
                    rnn_conv_w, rnn_conv_b, rnn_w_a, rnn_b_a, rnn_w_x, rnn_b_x, rnn_lam, w_out,
                    ffn_up, ffn_conv_w, ffn_down)
```

```python
import functools

import jax
import jax.numpy as jnp
from jax import lax
from jax.experimental import pallas as pl
from jax.experimental.pallas import tpu as pltpu

GRID_W = 64
N_RNN_HEADS = 16
RGLRU_C = 8.0
N_MOD = 6
EPS = 1e-6
RNN_CONV_W = 4
CONV_A_W = 3
FFN_CONV_W = 3

SUBLANES = 8
LANES = 128
MXU_WIDTH = 256
VMEM_BYTES = 64 * 1024 * 1024
HALO_ROWS = 16
MOD_ROWS = 32
NORM_UNROLL = 4

F32 = jnp.float32
BF16 = jnp.bfloat16


def _rms(x, g):
    var = jnp.mean(x * x, axis=-1, keepdims=True)
    return x * lax.rsqrt(var + EPS) * g


def _sigmoid(x):
    return 0.5 * jnp.tanh(0.5 * x) + 0.5


def _load_chunk(ref, c, ch):
    if len(ref.shape) == 2:
        return ref[pl.ds(pl.multiple_of(c * ch, ch), ch), :]
    r = ref.shape[1]
    t = ch // r
    return ref[pl.ds(pl.multiple_of(c * t, t), t), :, :].reshape(ch, ref.shape[2])


def _store_chunk(ref, c, ch, val):
    if len(ref.shape) == 2:
        ref[pl.ds(pl.multiple_of(c * ch, ch), ch), :] = val
    else:
        r = ref.shape[1]
        t = ch // r
        ref[pl.ds(pl.multiple_of(c * t, t), t), :, :] = val.reshape(t, r, ref.shape[2])


def _for_chunks(n, body, unroll=1):
    def wrapped(c, carry):
        body(c)
        return carry
    lax.fori_loop(0, n, wrapped, 0, unroll=unroll)


def _mod_body(cc_ref, w_ref, b_ref, o_ref):
    s = cc_ref[...]
    s = s * jax.nn.sigmoid(s)
    o_ref[...] = jnp.dot(s.astype(BF16), w_ref[...].astype(BF16),
                         preferred_element_type=F32) + b_ref[...]


def _modulation(cc, w_ada, b_ada, tn):
    depth, d, n = w_ada.shape
    rows = cc.shape[0]
    return pl.pallas_call(
        _mod_body,
        grid=(depth, n // tn),
        in_specs=[
            pl.BlockSpec((rows, d), lambda l, j: (0, 0)),
            pl.BlockSpec((None, d, tn), lambda l, j: (l, 0, j)),
            pl.BlockSpec((None, 1, tn), lambda l, j: (l, 0, j)),
        ],
        out_specs=pl.BlockSpec((None, rows, tn), lambda l, j: (l, 0, j)),
        out_shape=jax.ShapeDtypeStruct((depth, rows, n), F32),
        compiler_params=pltpu.CompilerParams(
            dimension_semantics=("arbitrary", "arbitrary"),
            vmem_limit_bytes=40 * 1024 * 1024),
        name="adaln_modulation",
    )(cc, w_ada, b_ada.reshape(depth, 1, n))


def _proj_body(x_ref, sc_ref, sh_ref, g_ref, w_ref, o_ref, u_ref, *, ch, tn):
    tm = u_ref.shape[0]

    def chunk(c):
        x = _load_chunk(x_ref, c, ch)
        u = _rms(x, g_ref[...]) * (1.0 + sc_ref[0:ch, :]) + sh_ref[0:ch, :]
        u_ref[pl.ds(pl.multiple_of(c * ch, ch), ch), :] = u.astype(BF16)

    _for_chunks(tm // ch, chunk, unroll=NORM_UNROLL)
    din = o_ref.shape[1]
    for lo in range(0, din, tn):
        hi = min(lo + tn, din)
        o_ref[:, lo:hi] = jnp.dot(u_ref[...], w_ref[:, lo:hi], preferred_element_type=F32).astype(o_ref.dtype)


def _x_spec(view, tile_rows, order=lambda i: i):
    if view["kind"] == "flat":
        return pl.BlockSpec((tile_rows, view["d"]), lambda i: (order(i), 0))
    r = view["r"]
    steps = tile_rows // r
    tpc = view["grid_rows"] // steps
    return pl.BlockSpec((steps, None, r, view["d"]), lambda i: (order(i) % tpc, order(i) // tpc, 0, 0))


def _as_view(x, view):
    if view["kind"] == "flat":
        return x
    return x.reshape(view["grid_rows"], GRID_W, view["r"], view["d"])


def _proj(x, view, sc, sh, g, w, tm, ch, tn):
    n, d = x.shape
    din = w.shape[1]
    const = lambda i: (0, 0)
    return pl.pallas_call(
        functools.partial(_proj_body, ch=ch, tn=tn),
        grid=(n // tm,),
        in_specs=[
            _x_spec(view, tm),
            pl.BlockSpec(sc.shape, const),
            pl.BlockSpec(sh.shape, const),
            pl.BlockSpec(g.shape, const),
            pl.BlockSpec(w.shape, const, pipeline_mode=pl.Buffered(1)),
        ],
        out_specs=pl.BlockSpec((tm, din), lambda i: (i, 0)),
        out_shape=jax.ShapeDtypeStruct((n, din), F32),
        scratch_shapes=[pltpu.VMEM((tm, d), BF16)],
        compiler_params=pltpu.CompilerParams(
            dimension_semantics=("arbitrary",),
            vmem_limit_bytes=56 * 1024 * 1024),
        name="in_proj",
    )(_as_view(x, view), sc, sh, g, w)


def _fill_rnn_params(par_ref, ba_ref, bx_ref, lam_ref):
    dr = par_ref.shape[1]
    par_ref[0:8, :] = jnp.broadcast_to(ba_ref[...], (SUBLANES, dr))
    par_ref[8:16, :] = jnp.broadcast_to(bx_ref[...], (SUBLANES, dr))
    par_ref[16:24, :] = jnp.broadcast_to(-RGLRU_C * jax.nn.softplus(-lam_ref[...]), (SUBLANES, dr))


def _rnn_conv(xe_ref, xc_ref, cw_ref, cb_ref, m, r, ch, reverse):
    def chunk(c):
        r0 = pl.multiple_of(c * ch, ch)
        acc = jnp.broadcast_to(cb_ref[...], (ch, xc_ref.shape[1]))
        for k in range(RNN_CONV_W):
            off = (RNN_CONV_W - 1 - k) * r if reverse else k * r
            acc = acc + cw_ref[k:k + 1, :] * xe_ref[pl.ds(r0 + off, ch), :]
        xc_ref[pl.ds(r0, ch), :] = acc

    _for_chunks(m // ch, chunk)


def _rnn_gates(xc_ref, wg_ref, pre_ref):
    dr = xc_ref.shape[1]
    nblk, gw, _ = wg_ref.shape
    for k in range(nblk):
        p = jnp.dot(xc_ref[:, k * gw:(k + 1) * gw].astype(BF16), wg_ref[k],
                    preferred_element_type=F32)
        pre_ref[:, k * gw:(k + 1) * gw] = p[:, :gw]
        pre_ref[:, dr + k * gw:dr + (k + 1) * gw] = p[:, gw:]


def _scan_step(row, h, xc_ref, pre_ref, par_ref, nb, emit):
    dr = xc_ref.shape[1]
    ng = dr // LANES
    new_h = []
    for sb in range(nb):
        rs = pl.ds(row + sb * SUBLANES, SUBLANES)
        for g in range(ng):
            cs = slice(g * LANES, (g + 1) * LANES)
            gate_r = _sigmoid(pre_ref[rs, cs] + par_ref[0:8, cs])
            gate_i = _sigmoid(pre_ref[rs, dr + g * LANES:dr + (g + 1) * LANES] + par_ref[8:16, cs])
            xc = xc_ref[rs, cs]
            log_a = par_ref[16:24, cs] * gate_r
            a = jnp.exp(log_a)
            v = jnp.sqrt(1.0 - a * a) * (gate_i * xc)
            hh = a * h[sb * ng + g] + v
            emit(rs, cs, hh)
            new_h.append(hh)
    return tuple(new_h)


def _load_state(hc_ref, nb):
    ng = hc_ref.shape[1] // LANES
    return tuple(hc_ref[sb * SUBLANES:(sb + 1) * SUBLANES, g * LANES:(g + 1) * LANES]
                 for sb in range(nb) for g in range(ng))


def _store_state(hc_ref, h, nb):
    ng = hc_ref.shape[1] // LANES
    for sb in range(nb):
        for g in range(ng):
            hc_ref[sb * SUBLANES:(sb + 1) * SUBLANES, g * LANES:(g + 1) * LANES] = h[sb * ng + g]


def _bwd_scan_body(xr_ref, cw_ref, cb_ref, wg_ref, ba_ref, bx_ref, lam_ref, h0_ref,
                   hb_ref, hl_ref,
                   xe_ref, xc_ref, pre_ref, par_ref, hc_ref, *, r, ch):
    m, dr = xr_ref.shape
    nb = r // SUBLANES
    steps = m // r
    halo = (RNN_CONV_W - 1) * r

    @pl.when(pl.program_id(0) == 0)
    def _init():
        xe_ref[m:m + halo, :] = jnp.zeros((halo, dr), F32)
        hc_ref[...] = h0_ref[...]
        _fill_rnn_params(par_ref, ba_ref, bx_ref, lam_ref)

    def copy(c):
        r0 = pl.multiple_of(c * ch, ch)
        xe_ref[pl.ds(r0, ch), :] = xr_ref[pl.ds(r0, ch), :].astype(F32)

    _for_chunks(m // ch, copy)
    _rnn_conv(xe_ref, xc_ref, cw_ref, cb_ref, m, r, ch, reverse=True)
    xe_ref[m:m + halo, :] = xe_ref[0:halo, :]
    _rnn_gates(xc_ref, wg_ref, pre_ref)

    def emit(rs, cs, hh):
        hb_ref[rs, cs] = hh

    def step(tt, h):
        row = pl.multiple_of((steps - 1 - tt) * r, r)
        return _scan_step(row, h, xc_ref, pre_ref, par_ref, nb, emit)

    h = lax.fori_loop(0, steps, step, _load_state(hc_ref, nb), unroll=2)
    _store_state(hc_ref, h, nb)
    hl_ref[...] = hc_ref[...]


def _bwd_scan(proj, rp, h0, r, m, ch):
    n = proj.shape[0]
    dr = h0.shape[1]
    nt = n // m
    xr_col = rp["xr_col"]
    const2 = lambda i: (0, 0)
    const3 = lambda i: (0, 0, 0)
    halo = (RNN_CONV_W - 1) * r
    return pl.pallas_call(
        functools.partial(_bwd_scan_body, r=r, ch=ch),
        grid=(nt,),
        in_specs=[
            pl.BlockSpec((m, dr), lambda i: (nt - 1 - i, xr_col)),
            pl.BlockSpec(rp["cw"].shape, const2),
            pl.BlockSpec(rp["cb"].shape, const2),
            pl.BlockSpec(rp["wg"].shape, const3),
            pl.BlockSpec(rp["ba"].shape, const2),
            pl.BlockSpec(rp["bx"].shape, const2),
            pl.BlockSpec(rp["lam"].shape, const2),
            pl.BlockSpec(h0.shape, const2),
        ],
        out_specs=[
            pl.BlockSpec((m, dr), lambda i: (nt - 1 - i, 0)),
            pl.BlockSpec(h0.shape, const2),
        ],
        out_shape=[
            jax.ShapeDtypeStruct((n, dr), F32),
            jax.ShapeDtypeStruct(h0.shape, F32),
        ],
        scratch_shapes=[
            pltpu.VMEM((m + halo, dr), F32),
            pltpu.VMEM((m, dr), F32),
            pltpu.VMEM((m, 2 * dr), F32),
            pltpu.VMEM((3 * SUBLANES, dr), F32),
            pltpu.VMEM(h0.shape, F32),
        ],
        compiler_params=pltpu.CompilerParams(
            dimension_semantics=("arbitrary",),
            vmem_limit_bytes=40 * 1024 * 1024),
        name="rglru_bwd_scan",
    )(proj, rp["cw"], rp["cb"], rp["wg"], rp["ba"], rp["bx"], rp["lam"], h0)


def _projscan_body(x_ref, sc_ref, sh_ref, g_ref, w_ref, cw_ref, cb_ref, wg_ref, ba_ref, bx_ref, lam_ref, h0_ref,
                   o_ref, hb_ref, hl_ref,
                   u_ref, xcar_ref, par_ref, hc_ref, *, r, mp, tn, xr_lo):
    tm, d = u_ref.shape
    dr = hb_ref.shape[1]
    din = o_ref.shape[1]
    nb = r // SUBLANES
    ng = dr // LANES
    nblk, gw, _ = wg_ref.shape
    halo = (RNN_CONV_W - 1) * r
    npieces = tm // mp

    @pl.when(pl.program_id(0) == 0)
    def _init():
        xcar_ref[...] = jnp.zeros((halo, dr), F32)
        hc_ref[...] = h0_ref[...]
        _fill_rnn_params(par_ref, ba_ref, bx_ref, lam_ref)

    sc = jnp.concatenate([sc_ref[...]] * (mp // MOD_ROWS), axis=0)
    sh = jnp.concatenate([sh_ref[...]] * (mp // MOD_ROWS), axis=0)
    xr = []
    for p in range(npieces):
        lo = p * mp
        u = (_rms(_rows(x_ref, lo, mp), g_ref[...]) * (1.0 + sc) + sh).astype(BF16)
        u_ref[lo:lo + mp, :] = u
        xr_p = jnp.dot(u, w_ref[:, xr_lo:xr_lo + dr], preferred_element_type=F32)
        o_ref[lo:lo + mp, xr_lo:xr_lo + dr] = xr_p
        xr.append(xr_p)

    for lo_c, hi_c in ((0, xr_lo), (xr_lo + dr, din)):
        for lo in range(lo_c, hi_c, tn):
            hi = min(lo + tn, hi_c)
            o_ref[:, lo:hi] = jnp.dot(u_ref[...], w_ref[:, lo:hi], preferred_element_type=F32)

    h = list(_load_state(hc_ref, nb))
    for p in reversed(range(npieces)):
        lo = p * mp
        later = xcar_ref[...] if p == npieces - 1 else xr[p + 1][0:halo, :]
        xe = jnp.concatenate([xr[p], later], axis=0)
        xc = cb_ref[...] + cw_ref[RNN_CONV_W - 1:RNN_CONV_W, :] * xe[0:mp, :]
        for k in range(RNN_CONV_W - 1):
            off = (RNN_CONV_W - 1 - k) * r
            xc = xc + cw_ref[k:k + 1, :] * xe[off:off + mp, :]
        pre = [jnp.dot(xc[:, k * gw:(k + 1) * gw].astype(BF16), wg_ref[k], preferred_element_type=F32)
               for k in range(nblk)]
        for t in reversed(range(mp // r)):
            for sb in range(nb):
                lr = t * r + sb * SUBLANES
                rs = slice(lo + lr, lo + lr + SUBLANES)
                for g in range(ng):
                    cs = slice(g * LANES, (g + 1) * LANES)
                    k, off = divmod(g * LANES, gw)
                    hh = _rglru_unit(pre[k][lr:lr + SUBLANES, off:off + LANES],
                                     pre[k][lr:lr + SUBLANES, gw + off:gw + off + LANES],
                                     xc[lr:lr + SUBLANES, cs],
                                     par_ref[0:8, cs], par_ref[8:16, cs], par_ref[16:24, cs],
                                     h[sb * ng + g])
                    h[sb * ng + g] = hh
                    hb_ref[rs, cs] = hh

    xcar_ref[...] = xr[0][0:halo, :]
    _store_state(hc_ref, h, nb)
    hl_ref[...] = hc_ref[...]


def _projscan(x, view, sc, sh, g, w, rp, h0, r, tm, mp, tn, xr_lo):
    n, d = x.shape
    din = w.shape[1]
    dr = h0.shape[1]
    nt = n // tm
    const2 = lambda i: (0, 0)
    const3 = lambda i: (0, 0, 0)
    halo = (RNN_CONV_W - 1) * r
    return pl.pallas_call(
        functools.partial(_projscan_body, r=r, mp=mp, tn=tn, xr_lo=xr_lo),
        grid=(nt,),
        in_specs=[
            _x_spec(view, tm, order=lambda i: nt - 1 - i),
            pl.BlockSpec(sc.shape, const2),
            pl.BlockSpec(sh.shape, const2),
            pl.BlockSpec(g.shape, const2),
            pl.BlockSpec(w.shape, const2, pipeline_mode=pl.Buffered(1)),
            pl.BlockSpec(rp["cw"].shape, const2),
            pl.BlockSpec(rp["cb"].shape, const2),
            pl.BlockSpec(rp["wg"].shape, const3),
            pl.BlockSpec(rp["ba"].shape, const2),
            pl.BlockSpec(rp["bx"].shape, const2),
            pl.BlockSpec(rp["lam"].shape, const2),
            pl.BlockSpec(h0.shape, const2),
        ],
        out_specs=[
            pl.BlockSpec((tm, din), lambda i: (nt - 1 - i, 0)),
            pl.BlockSpec((tm, dr), lambda i: (nt - 1 - i, 0)),
            pl.BlockSpec(h0.shape, const2),
        ],
        out_shape=[
            jax.ShapeDtypeStruct((n, din), F32),
            jax.ShapeDtypeStruct((n, dr), F32),
            jax.ShapeDtypeStruct(h0.shape, F32),
        ],
        scratch_shapes=[
            pltpu.VMEM((tm, d), BF16),
            pltpu.VMEM((halo, dr), F32),
            pltpu.VMEM((3 * SUBLANES, dr), F32),
            pltpu.VMEM(h0.shape, F32),
        ],
        compiler_params=pltpu.CompilerParams(
            dimension_semantics=("arbitrary",),
            vmem_limit_bytes=56 * 1024 * 1024),
        name="proj_bwd_scan",
    )(_as_view(x, view), sc, sh, g, w, rp["cw"], rp["cb"], rp["wg"], rp["ba"], rp["bx"], rp["lam"], h0)


def _rows(ref, lo, n):
    if len(ref.shape) == 2:
        return ref[lo:lo + n, :]
    r = ref.shape[1]
    return ref[lo // r:(lo + n) // r, :, :].reshape(n, ref.shape[2])


def _set_rows(ref, lo, n, val):
    if len(ref.shape) == 2:
        ref[lo:lo + n, :] = val
    else:
        r = ref.shape[1]
        ref[lo // r:(lo + n) // r, :, :] = val.reshape(n // r, r, ref.shape[2])


def _rglru_unit(pre_r, pre_i, xc, ba, bx, c8, h):
    gate_r = _sigmoid(pre_r + ba)
    gate_i = _sigmoid(pre_i + bx)
    a = jnp.exp(c8 * gate_r)
    v = jnp.sqrt(1.0 - a * a) * (gate_i * xc)
    return a * h + v


def _mix_body(bg_ref, cg_ref, xa_ref, xr_ref, gr_ref, cgh_ref, xah_ref, hb_ref, x_ref,
              cw_ref, cb_ref, wg_ref, ba_ref, bx_ref, lam_ref, h0_ref,
              caw_ref, wout_ref, gn_ref, gate_ref,
              o_ref, hl_ref,
              xcar_ref, zcar_ref, par_ref, hc_ref, y_ref, *, r, mp):
    m, dr = xr_ref.shape
    dc = bg_ref.shape[1]
    nb = r // SUBLANES
    ng = dr // LANES
    nblk, gw, _ = wg_ref.shape
    halo = (RNN_CONV_W - 1) * r
    npieces = m // mp
    i = pl.program_id(0)
    last = pl.num_programs(0) - 1

    @pl.when(i == 0)
    def _init():
        xcar_ref[...] = jnp.zeros((halo, dr), F32)
        zcar_ref[...] = jnp.zeros((r, dc), F32)
        hc_ref[...] = h0_ref[...]
        _fill_rnn_params(par_ref, ba_ref, bx_ref, lam_ref)

    f32 = lambda ref, lo, n: ref[lo:lo + n, :].astype(F32)
    gate_rows = jnp.concatenate([gate_ref[...]] * (mp // MOD_ROWS), axis=0)
    h = list(_load_state(hc_ref, nb))
    for p in range(npieces):
        lo = p * mp
        if p == 0:
            xe = jnp.concatenate([xcar_ref[...], f32(xr_ref, 0, mp)], axis=0)
        else:
            xe = f32(xr_ref, lo - halo, mp + halo)
        xc = cb_ref[...] + cw_ref[0:1, :] * xe[0:mp, :]
        for k in range(1, RNN_CONV_W):
            xc = xc + cw_ref[k:k + 1, :] * xe[k * r:k * r + mp, :]
        pre = [jnp.dot(xc[:, k * gw:(k + 1) * gw].astype(BF16), wg_ref[k], preferred_element_type=F32)
               for k in range(nblk)]
        for t in range(mp // r):
            for sb in range(nb):
                lr = t * r + sb * SUBLANES
                rs = slice(lo + lr, lo + lr + SUBLANES)
                for g in range(ng):
                    cs = slice(g * LANES, (g + 1) * LANES)
                    k, off = divmod(g * LANES, gw)
                    hh = _rglru_unit(pre[k][lr:lr + SUBLANES, off:off + LANES],
                                     pre[k][lr:lr + SUBLANES, gw + off:gw + off + LANES],
                                     xc[lr:lr + SUBLANES, cs],
                                     par_ref[0:8, cs], par_ref[8:16, cs], par_ref[16:24, cs],
                                     h[sb * ng + g])
                    h[sb * ng + g] = hh
                    y_ref[rs, dc + g * LANES:dc + (g + 1) * LANES] = (
                        (hh + hb_ref[rs, cs]) * jax.nn.gelu(gr_ref[rs, cs].astype(F32)))

        z_prev = zcar_ref[...] if p == 0 else f32(cg_ref, lo - r, r) * f32(xa_ref, lo - r, r)
        if p == npieces - 1:
            z_next = jnp.where(i < last, f32(cgh_ref, 0, r) * f32(xah_ref, 0, r), 0.0)
        else:
            z_next = f32(cg_ref, lo + mp, r) * f32(xa_ref, lo + mp, r)
        ze = jnp.concatenate([z_prev, f32(cg_ref, lo, mp) * f32(xa_ref, lo, mp), z_next], axis=0)
        conv = (caw_ref[0:1, :] * ze[0:mp, :] + caw_ref[1:2, :] * ze[r:r + mp, :]
                + caw_ref[2:3, :] * ze[2 * r:2 * r + mp, :])
        y_ref[lo:lo + mp, 0:dc] = f32(bg_ref, lo, mp) * conv

        yo = jnp.dot(y_ref[lo:lo + mp, :].astype(BF16), wout_ref[...], preferred_element_type=F32)
        _set_rows(o_ref, lo, mp, _rows(x_ref, lo, mp) + gate_rows * _rms(yo, gn_ref[...]))

    xcar_ref[...] = f32(xr_ref, m - halo, halo)
    zcar_ref[...] = f32(cg_ref, m - r, r) * f32(xa_ref, m - r, r)
    _store_state(hc_ref, h, nb)
    hl_ref[...] = hc_ref[...]


def _mix(proj, hb, x, view, rp, h0, caw, wout, gn, gate, r, m, mp, cols):
    n, d = x.shape
    dr = h0.shape[1]
    dc = caw.shape[1]
    dmix = wout.shape[0]
    nt = n // m
    hr = r
    nh = n // hr
    const2 = lambda i: (0, 0)
    const3 = lambda i: (0, 0, 0)
    halo = (RNN_CONV_W - 1) * r
    col = lambda k: pl.BlockSpec((m, dc), lambda i: (i, k))
    nxt = lambda k: pl.BlockSpec((hr, dc), lambda i: (jnp.minimum((i + 1) * (m // hr), nh - 1), k))
    xspec = _x_spec(view, m)
    return pl.pallas_call(
        functools.partial(_mix_body, r=r, mp=mp),
        grid=(nt,),
        in_specs=[
            col(cols["bg"]), col(cols["cg"]), col(cols["xa"]), col(cols["xr"]), col(cols["gr"]),
            nxt(cols["cg"]), nxt(cols["xa"]),
            pl.BlockSpec((m, dr), lambda i: (i, 0)),
            xspec,
            pl.BlockSpec(rp["cw"].shape, const2),
            pl.BlockSpec(rp["cb"].shape, const2),
            pl.BlockSpec(rp["wg"].shape, const3),
            pl.BlockSpec(rp["ba"].shape, const2),
            pl.BlockSpec(rp["bx"].shape, const2),
            pl.BlockSpec(rp["lam"].shape, const2),
            pl.BlockSpec(h0.shape, const2),
            pl.BlockSpec(caw.shape, const2),
            pl.BlockSpec(wout.shape, const2, pipeline_mode=pl.Buffered(1)),
            pl.BlockSpec(gn.shape, const2),
            pl.BlockSpec(gate.shape, const2),
        ],
        out_specs=[
            xspec,
            pl.BlockSpec(h0.shape, const2),
        ],
        out_shape=[
            jax.ShapeDtypeStruct(_as_view(x, view).shape, F32),
            jax.ShapeDtypeStruct(h0.shape, F32),
        ],
        scratch_shapes=[
            pltpu.VMEM((halo, dr), F32),
            pltpu.VMEM((r, dc), F32),
            pltpu.VMEM((3 * SUBLANES, dr), F32),
            pltpu.VMEM(h0.shape, F32),
            pltpu.VMEM((m, dmix), F32),
        ],
        compiler_params=pltpu.CompilerParams(
            dimension_semantics=("arbitrary",),
            vmem_limit_bytes=56 * 1024 * 1024),
        name="mix_fwd",
    )(proj, proj, proj, proj, proj, proj, proj, hb, _as_view(x, view),
      rp["cw"], rp["cb"], rp["wg"], rp["ba"], rp["bx"], rp["lam"], h0, caw, wout, gn, gate)


def _ffn_body(x_ref, xp_ref, xn_ref, sc_ref, sh_ref, gate_ref, gin_ref, gout_ref,
              wg_ref, wv_ref, cg_ref, cv_ref, wd_ref, o_ref, u_ref, acc_ref, *, r, ch, ts):
    tm, d = x_ref.shape
    tf = wd_ref.shape[0]
    i = pl.program_id(0)
    j = pl.program_id(1)
    ni = pl.num_programs(0)
    nj = pl.num_programs(1)
    hal = HALO_ROWS

    @pl.when(j == 0)
    def _norm():
        def chunk(c):
            x = _load_chunk(x_ref, c, ch)
            u = _rms(x, gin_ref[...]) * (1.0 + sc_ref[0:ch, :]) + sh_ref[0:ch, :]
            u_ref[pl.ds(pl.multiple_of(hal + c * ch, ch), ch), :] = u.astype(BF16)
            acc_ref[pl.ds(pl.multiple_of(c * ch, ch), ch), :] = jnp.zeros((ch, d), F32)

        _for_chunks(tm // ch, chunk, unroll=NORM_UNROLL)
        for ref, lo, inside in ((xp_ref, 0, i > 0), (xn_ref, hal + tm, i < ni - 1)):
            u = _rms(ref[...], gin_ref[...]) * (1.0 + sc_ref[0:hal, :]) + sh_ref[0:hal, :]
            u_ref[lo:lo + hal, :] = jnp.where(inside, u, 0.0).astype(BF16)

    u = u_ref[...]
    acc = None
    def hidden(w_ref, c_ref, cols):
        h = jnp.dot(u, w_ref[:, cols], preferred_element_type=F32)
        cw = c_ref[:, cols]
        return (cw[0:1, :] * h[hal - r:hal - r + tm, :] + cw[1:2, :] * h[hal:hal + tm, :]
                + cw[2:3, :] * h[hal + r:hal + r + tm, :])

    for k in range(tf // ts):
        cols = slice(ts * k, ts * (k + 1))
        a = (jax.nn.gelu(hidden(wg_ref, cg_ref, cols)) * hidden(wv_ref, cv_ref, cols)).astype(BF16)
        p = jnp.dot(a, wd_ref[cols, :], preferred_element_type=F32)
        acc = p if acc is None else acc + p
    acc_ref[...] += acc

    @pl.when(j == nj - 1)
    def _finish():
        def chunk(c):
            rows = pl.ds(pl.multiple_of(c * ch, ch), ch)
            o_ref[rows, :] = x_ref[rows, :] + gate_ref[0:ch, :] * _rms(acc_ref[rows, :], gout_ref[...])

        _for_chunks(tm // ch, chunk, unroll=NORM_UNROLL)


def _ffn(x, sc, sh, gate, gin, gout, wup, cw, wd, r, tm, tf, ts, ch):
    n, d = x.shape
    dff = wd.shape[0]
    ni = n // tm
    nj = dff // tf
    nh = n // HALO_ROWS
    const = lambda i, j: (0, 0)
    return pl.pallas_call(
        functools.partial(_ffn_body, r=r, ch=ch, ts=ts),
        grid=(ni, nj),
        in_specs=[
            pl.BlockSpec((tm, d), lambda i, j: (i, 0)),
            pl.BlockSpec((HALO_ROWS, d), lambda i, j: (jnp.maximum(i * (tm // HALO_ROWS) - 1, 0), 0)),
            pl.BlockSpec((HALO_ROWS, d), lambda i, j: (jnp.minimum((i + 1) * (tm // HALO_ROWS), nh - 1), 0)),
            pl.BlockSpec(sc.shape, const),
            pl.BlockSpec(sh.shape, const),
            pl.BlockSpec(gate.shape, const),
            pl.BlockSpec(gin.shape, const),
            pl.BlockSpec(gout.shape, const),
            pl.BlockSpec((d, tf), lambda i, j: (0, j)),
            pl.BlockSpec((d, tf), lambda i, j: (0, nj + j)),
            pl.BlockSpec((FFN_CONV_W, tf), lambda i, j: (0, j)),
            pl.BlockSpec((FFN_CONV_W, tf), lambda i, j: (0, nj + j)),
            pl.BlockSpec((tf, d), lambda i, j: (j, 0)),
        ],
        out_specs=pl.BlockSpec((tm, d), lambda i, j: (i, 0)),
        out_shape=jax.ShapeDtypeStruct((n, d), F32),
        scratch_shapes=[
            pltpu.VMEM((tm + 2 * HALO_ROWS, d), BF16),
            pltpu.VMEM((tm, d), F32),
        ],
        compiler_params=pltpu.CompilerParams(
            dimension_semantics=("arbitrary", "arbitrary"),
            vmem_limit_bytes=56 * 1024 * 1024),
        name="conv_ffn",
    )(x, x, x, sc, sh, gate, gin, gout, wup, wup, cw, cw, wd)


def _block_diag(w, gw):
    heads, hd, _ = w.shape
    hpb = gw // hd
    w = w.reshape(heads // hpb, hpb, hd, hd)
    eye = jnp.eye(hpb, dtype=w.dtype)
    return jnp.einsum("nhij,hg->nhigj", w, eye).reshape(heads // hpb, gw, gw)


def _plan(d, dr, dff, n_rows, r, grid_rows):
    seq_rows = grid_rows * r if grid_rows else n_rows
    return dict(
        proj_rows=min(256, seq_rows),
        mix_rows=min(256, seq_rows),
        ffn_rows=min(512, n_rows),
        ffn_cols=min(512, dff),
        wide_chunk=2 * SUBLANES,
        mix_chunk=4 * SUBLANES,
        mix_piece=min(128, seq_rows),
        ffn_sub=min(MXU_WIDTH, dff),
        proj_cols=min(1024, 3 * (d - dr) + 2 * dr),
    )


def _tile_rows(mod, r, rows):
    return jnp.tile(mod, (rows // r, 1))


def _layer(x, r, grid_rows, mods, h0, lw, plan):
    n, d = x.shape
    sh1, sc1, g1, sh2, sc2, g2 = [_tile_rows(m, r, MOD_ROWS) for m in jnp.split(mods, N_MOD, axis=-1)]
    if grid_rows:
        view = dict(kind="grid", r=r, d=d, grid_rows=grid_rows)
    else:
        view = dict(kind="flat", d=d)
    ng = lw["norm_g"]
    dr = h0.shape[-1]
    proj, hb, last_b = _projscan(x, view, sc1, sh1, ng[0:1], lw["w_in"], lw["rnn"][1], h0[:, 1], r,
                                 plan["proj_rows"], plan["mix_piece"], plan["proj_cols"], lw["cols"]["xr"] * dr)
    x, last_f = _mix(proj, hb, x, view, lw["rnn"][0], h0[:, 0], lw["conv_a_w"], lw["w_out"],
                     ng[1:2], g1, r, plan["mix_rows"], plan["mix_piece"], lw["cols"])
    x = x.reshape(n, d)
    x = _ffn(x, sc2, sh2, g2, ng[2:3], ng[3:4], lw["ffn_up"], lw["ffn_conv_w"], lw["ffn_down"],
             r, plan["ffn_rows"], plan["ffn_cols"], plan["ffn_sub"], plan["wide_chunk"])
    return x, jnp.stack([last_f, last_b], axis=1)


def _forward(x_prompt, x_sample, state_h, c, c_ctx, w_ada, b_ada, norm_g, w_in, conv_a_w,
             rnn_conv_w, rnn_conv_b, rnn_w_a, rnn_b_a, rnn_w_x, rnn_b_x, rnn_lam, w_out,
             ffn_up, ffn_conv_w, ffn_down, plan_overrides=None):
    b, s, d = x_prompt.shape
    bd, l, _ = x_sample.shape
    depth = w_ada.shape[0]
    dr = rnn_lam.shape[-1]
    dc = conv_a_w.shape[-1]
    dff = ffn_down.shape[1]
    grid_rows = l // GRID_W
    gw = min(MXU_WIDTH, dr)

    plan_p = _plan(d, dr, dff, s * b, b, 0)
    plan_s = _plan(d, dr, dff, l * bd, bd, grid_rows)
    for p in (plan_p, plan_s):
        p.update(plan_overrides or {})

    xp = jnp.transpose(x_prompt, (1, 0, 2)).reshape(s * b, d)
    xs = jnp.transpose(x_sample, (1, 0, 2)).reshape(l * bd, d)

    cc = jnp.concatenate([c, jnp.broadcast_to(c_ctx[None, :], (SUBLANES, d))], axis=0)
    mods = _modulation(cc, w_ada, b_ada, min(1024, N_MOD * d))

    h_zero = jnp.zeros((b, 2, dr), F32)
    states = []
    for li in range(depth):
        rnn = []
        for di in range(2):
            wg = jnp.concatenate([_block_diag(rnn_w_a[li, di], gw), _block_diag(rnn_w_x[li, di], gw)],
                                 axis=-1).astype(BF16)
            rnn.append(dict(cw=rnn_conv_w[li, di], cb=rnn_conv_b[li, di][None, :], wg=wg,
                            ba=rnn_b_a[li, di][None, :], bx=rnn_b_x[li, di][None, :],
                            lam=rnn_lam[li, di][None, :], xr_col=3 * dc // dr))
        lw = dict(
            norm_g=norm_g[li],
            w_in=w_in[li].astype(BF16),
            conv_a_w=conv_a_w[li],
            rnn=rnn,
            w_out=w_out[li].astype(BF16),
            ffn_up=ffn_up[li].astype(BF16),
            ffn_conv_w=ffn_conv_w[li],
            ffn_down=ffn_down[li].astype(BF16),
            cols=dict(bg=0, cg=1, xa=2, xr=3 * dc // dr, gr=3 * dc // dr + 1),
        )
        mod_ctx = jnp.broadcast_to(mods[li, bd:bd + 1], (b, N_MOD * d))
        xp, st = _layer(xp, b, 0, mod_ctx, h_zero, lw, plan_p)
        states.append(st)
        xs, _ = _layer(xs, bd, grid_rows if li % 2 == 1 else 0, mods[li, :bd], state_h[:, li], lw, plan_s)

    y_prompt = jnp.transpose(xp.reshape(s, b, d), (1, 0, 2))
    y_sample = jnp.transpose(xs.reshape(l, bd, d), (1, 0, 2))
    return y_prompt, y_sample, jnp.stack(states, axis=1)


def kernel(x_prompt, x_sample, state_h, c, c_ctx, w_ada, b_ada, norm_g, w_in, conv_a_w, rnn_conv_w,
           rnn_conv_b, rnn_w_a, rnn_b_a, rnn_w_x, rnn_b_x, rnn_lam, w_out, ffn_up, ffn_conv_w, ffn_down):
    return _forward(x_prompt, x_sample, state_h, c, c_ctx, w_ada, b_ada, norm_g, w_in, conv_a_w,
                    rnn_conv_w, rnn_conv_b, rnn_w_a, rnn_b_a, rnn_w_x, rnn_b_x, rnn_lam, w_out,
                    ffn_up, ffn_conv_w, ffn_down)
```

```python
import functools

import jax
import jax.numpy as jnp
from jax import lax
from jax.experimental import pallas as pl
from jax.experimental.pallas import tpu as pltpu

GRID_W = 64
N_RNN_HEADS = 16
RGLRU_C = 8.0
N_MOD = 6
EPS = 1e-6
RNN_CONV_W = 4
CONV_A_W = 3
FFN_CONV_W = 3

SUBLANES = 8
LANES = 128
MXU_WIDTH = 256
VMEM_BYTES = 64 * 1024 * 1024
HALO_ROWS = 16
MOD_ROWS = 32
NORM_UNROLL = 4

F32 = jnp.float32
BF16 = jnp.bfloat16


def _rms(x, g):
    var = jnp.mean(x * x, axis=-1, keepdims=True)
    return x * lax.rsqrt(var + EPS) * g


def _sigmoid(x):
    return 0.5 * jnp.tanh(0.5 * x) + 0.5


def _load_chunk(ref, c, ch):
    if len(ref.shape) == 2:
        return ref[pl.ds(pl.multiple_of(c * ch, ch), ch), :]
    r = ref.shape[1]
    t = ch // r
    return ref[pl.ds(pl.multiple_of(c * t, t), t), :, :].reshape(ch, ref.shape[2])


def _store_chunk(ref, c, ch, val):
    if len(ref.shape) == 2:
        ref[pl.ds(pl.multiple_of(c * ch, ch), ch), :] = val
    else:
        r = ref.shape[1]
        t = ch // r
        ref[pl.ds(pl.multiple_of(c * t, t), t), :, :] = val.reshape(t, r, ref.shape[2])


def _for_chunks(n, body, unroll=1):
    def wrapped(c, carry):
        body(c)
        return carry
    lax.fori_loop(0, n, wrapped, 0, unroll=unroll)


def _mod_body(cc_ref, w_ref, b_ref, o_ref):
    s = cc_ref[...]
    s = s * jax.nn.sigmoid(s)
    o_ref[...] = jnp.dot(s.astype(BF16), w_ref[...].astype(BF16),
                         preferred_element_type=F32) + b_ref[...]


def _modulation(cc, w_ada, b_ada, tn):
    depth, d, n = w_ada.shape
    rows = cc.shape[0]
    return pl.pallas_call(
        _mod_body,
        grid=(depth, n // tn),
        in_specs=[
            pl.BlockSpec((rows, d), lambda l, j: (0, 0)),
            pl.BlockSpec((None, d, tn), lambda l, j: (l, 0, j)),
            pl.BlockSpec((None, 1, tn), lambda l, j: (l, 0, j)),
        ],
        out_specs=pl.BlockSpec((None, rows, tn), lambda l, j: (l, 0, j)),
        out_shape=jax.ShapeDtypeStruct((depth, rows, n), F32),
        compiler_params=pltpu.CompilerParams(
            dimension_semantics=("arbitrary", "arbitrary"),
            vmem_limit_bytes=40 * 1024 * 1024),
        name="adaln_modulation",
    )(cc, w_ada, b_ada.reshape(depth, 1, n))


def _proj_body(x_ref, sc_ref, sh_ref, g_ref, w_ref, o_ref, u_ref, *, ch, tn):
    tm = u_ref.shape[0]

    def chunk(c):
        x = _load_chunk(x_ref, c, ch)
        u = _rms(x, g_ref[...]) * (1.0 + sc_ref[0:ch, :]) + sh_ref[0:ch, :]
        u_ref[pl.ds(pl.multiple_of(c * ch, ch), ch), :] = u.astype(BF16)

    _for_chunks(tm // ch, chunk, unroll=NORM_UNROLL)
    din = o_ref.shape[1]
    for lo in range(0, din, tn):
        hi = min(lo + tn, din)
        o_ref[:, lo:hi] = jnp.dot(u_ref[...], w_ref[:, lo:hi], preferred_element_type=F32).astype(o_ref.dtype)


def _x_spec(view, tile_rows, order=lambda i: i):
    if view["kind"] == "flat":
        return pl.BlockSpec((tile_rows, view["d"]), lambda i: (order(i), 0))
    r = view["r"]
    steps = tile_rows // r
    tpc = view["grid_rows"] // steps
    return pl.BlockSpec((steps, None, r, view["d"]), lambda i: (order(i) % tpc, order(i) // tpc, 0, 0))


def _as_view(x, view):
    if view["kind"] == "flat":
        return x
    return x.reshape(view["grid_rows"], GRID_W, view["r"], view["d"])


def _proj(x, view, sc, sh, g, w, tm, ch, tn):
    n, d = x.shape
    din = w.shape[1]
    const = lambda i: (0, 0)
    return pl.pallas_call(
        functools.partial(_proj_body, ch=ch, tn=tn),
        grid=(n // tm,),
        in_specs=[
            _x_spec(view, tm),
            pl.BlockSpec(sc.shape, const),
            pl.BlockSpec(sh.shape, const),
            pl.BlockSpec(g.shape, const),
            pl.BlockSpec(w.shape, const, pipeline_mode=pl.Buffered(1)),
        ],
        out_specs=pl.BlockSpec((tm, din), lambda i: (i, 0)),
        out_shape=jax.ShapeDtypeStruct((n, din), F32),
        scratch_shapes=[pltpu.VMEM((tm, d), BF16)],
        compiler_params=pltpu.CompilerParams(
            dimension_semantics=("arbitrary",),
            vmem_limit_bytes=56 * 1024 * 1024),
        name="in_proj",
    )(_as_view(x, view), sc, sh, g, w)


def _fill_rnn_params(par_ref, ba_ref, bx_ref, lam_ref):
    dr = par_ref.shape[1]
    par_ref[0:8, :] = jnp.broadcast_to(ba_ref[...], (SUBLANES, dr))
    par_ref[8:16, :] = jnp.broadcast_to(bx_ref[...], (SUBLANES, dr))
    par_ref[16:24, :] = jnp.broadcast_to(-RGLRU_C * jax.nn.softplus(-lam_ref[...]), (SUBLANES, dr))


def _rnn_conv(xe_ref, xc_ref, cw_ref, cb_ref, m, r, ch, reverse):
    def chunk(c):
        r0 = pl.multiple_of(c * ch, ch)
        acc = jnp.broadcast_to(cb_ref[...], (ch, xc_ref.shape[1]))
        for k in range(RNN_CONV_W):
            off = (RNN_CONV_W - 1 - k) * r if reverse else k * r
            acc = acc + cw_ref[k:k + 1, :] * xe_ref[pl.ds(r0 + off, ch), :]
        xc_ref[pl.ds(r0, ch), :] = acc

    _for_chunks(m // ch, chunk)


def _rnn_gates(xc_ref, wg_ref, pre_ref):
    dr = xc_ref.shape[1]
    nblk, gw, _ = wg_ref.shape
    for k in range(nblk):
        p = jnp.dot(xc_ref[:, k * gw:(k + 1) * gw].astype(BF16), wg_ref[k],
                    preferred_element_type=F32)
        pre_ref[:, k * gw:(k + 1) * gw] = p[:, :gw]
        pre_ref[:, dr + k * gw:dr + (k + 1) * gw] = p[:, gw:]


def _scan_step(row, h, xc_ref, pre_ref, par_ref, nb, emit):
    dr = xc_ref.shape[1]
    ng = dr // LANES
    new_h = []
    for sb in range(nb):
        rs = pl.ds(row + sb * SUBLANES, SUBLANES)
        for g in range(ng):
            cs = slice(g * LANES, (g + 1) * LANES)
            gate_r = _sigmoid(pre_ref[rs, cs] + par_ref[0:8, cs])
            gate_i = _sigmoid(pre_ref[rs, dr + g * LANES:dr + (g + 1) * LANES] + par_ref[8:16, cs])
            xc = xc_ref[rs, cs]
            log_a = par_ref[16:24, cs] * gate_r
            a = jnp.exp(log_a)
            v = jnp.sqrt(1.0 - a * a) * (gate_i * xc)
            hh = a * h[sb * ng + g] + v
            emit(rs, cs, hh)
            new_h.append(hh)
    return tuple(new_h)


def _load_state(hc_ref, nb):
    ng = hc_ref.shape[1] // LANES
    return tuple(hc_ref[sb * SUBLANES:(sb + 1) * SUBLANES, g * LANES:(g + 1) * LANES]
                 for sb in range(nb) for g in range(ng))


def _store_state(hc_ref, h, nb):
    ng = hc_ref.shape[1] // LANES
    for sb in range(nb):
        for g in range(ng):
            hc_ref[sb * SUBLANES:(sb + 1) * SUBLANES, g * LANES:(g + 1) * LANES] = h[sb * ng + g]


def _bwd_scan_body(xr_ref, cw_ref, cb_ref, wg_ref, ba_ref, bx_ref, lam_ref, h0_ref,
                   hb_ref, hl_ref,
                   xe_ref, xc_ref, pre_ref, par_ref, hc_ref, *, r, ch):
    m, dr = xr_ref.shape
    nb = r // SUBLANES
    steps = m // r
    halo = (RNN_CONV_W - 1) * r

    @pl.when(pl.program_id(0) == 0)
    def _init():
        xe_ref[m:m + halo, :] = jnp.zeros((halo, dr), F32)
        hc_ref[...] = h0_ref[...]
        _fill_rnn_params(par_ref, ba_ref, bx_ref, lam_ref)

    def copy(c):
        r0 = pl.multiple_of(c * ch, ch)
        xe_ref[pl.ds(r0, ch), :] = xr_ref[pl.ds(r0, ch), :].astype(F32)

    _for_chunks(m // ch, copy)
    _rnn_conv(xe_ref, xc_ref, cw_ref, cb_ref, m, r, ch, reverse=True)
    xe_ref[m:m + halo, :] = xe_ref[0:halo, :]
    _rnn_gates(xc_ref, wg_ref, pre_ref)

    def emit(rs, cs, hh):
        hb_ref[rs, cs] = hh

    def step(tt, h):
        row = pl.multiple_of((steps - 1 - tt) * r, r)
        return _scan_step(row, h, xc_ref, pre_ref, par_ref, nb, emit)

    h = lax.fori_loop(0, steps, step, _load_state(hc_ref, nb), unroll=2)
    _store_state(hc_ref, h, nb)
    hl_ref[...] = hc_ref[...]


def _bwd_scan(proj, rp, h0, r, m, ch):
    n = proj.shape[0]
    dr = h0.shape[1]
    nt = n // m
    xr_col = rp["xr_col"]
    const2 = lambda i: (0, 0)
    const3 = lambda i: (0, 0, 0)
    halo = (RNN_CONV_W - 1) * r
    return pl.pallas_call(
        functools.partial(_bwd_scan_body, r=r, ch=ch),
        grid=(nt,),
        in_specs=[
            pl.BlockSpec((m, dr), lambda i: (nt - 1 - i, xr_col)),
            pl.BlockSpec(rp["cw"].shape, const2),
            pl.BlockSpec(rp["cb"].shape, const2),
            pl.BlockSpec(rp["wg"].shape, const3),
            pl.BlockSpec(rp["ba"].shape, const2),
            pl.BlockSpec(rp["bx"].shape, const2),
            pl.BlockSpec(rp["lam"].shape, const2),
            pl.BlockSpec(h0.shape, const2),
        ],
        out_specs=[
            pl.BlockSpec((m, dr), lambda i: (nt - 1 - i, 0)),
            pl.BlockSpec(h0.shape, const2),
        ],
        out_shape=[
            jax.ShapeDtypeStruct((n, dr), F32),
            jax.ShapeDtypeStruct(h0.shape, F32),
        ],
        scratch_shapes=[
            pltpu.VMEM((m + halo, dr), F32),
            pltpu.VMEM((m, dr), F32),
            pltpu.VMEM((m, 2 * dr), F32),
            pltpu.VMEM((3 * SUBLANES, dr), F32),
            pltpu.VMEM(h0.shape, F32),
        ],
        compiler_params=pltpu.CompilerParams(
            dimension_semantics=("arbitrary",),
            vmem_limit_bytes=40 * 1024 * 1024),
        name="rglru_bwd_scan",
    )(proj, rp["cw"], rp["cb"], rp["wg"], rp["ba"], rp["bx"], rp["lam"], h0)


def _projscan_body(x_ref, sc_ref, sh_ref, g_ref, w_ref, cw_ref, cb_ref, wg_ref, ba_ref, bx_ref, lam_ref, h0_ref,
                   o_ref, hb_ref, hl_ref,
                   u_ref, xcar_ref, par_ref, hc_ref, *, r, mp, tn, xr_lo):
    tm, d = u_ref.shape
    dr = hb_ref.shape[1]
    din = o_ref.shape[1]
    nb = r // SUBLANES
    ng = dr // LANES
    nblk, gw, _ = wg_ref.shape
    halo = (RNN_CONV_W - 1) * r
    npieces = tm // mp

    @pl.when(pl.program_id(0) == 0)
    def _init():
        xcar_ref[...] = jnp.zeros((halo, dr), F32)
        hc_ref[...] = h0_ref[...]
        _fill_rnn_params(par_ref, ba_ref, bx_ref, lam_ref)

    sc = jnp.concatenate([sc_ref[...]] * (mp // MOD_ROWS), axis=0)
    sh = jnp.concatenate([sh_ref[...]] * (mp // MOD_ROWS), axis=0)
    xr = []
    for p in range(npieces):
        lo = p * mp
        u = (_rms(_rows(x_ref, lo, mp), g_ref[...]) * (1.0 + sc) + sh).astype(BF16)
        u_ref[lo:lo + mp, :] = u
        xr_p = jnp.dot(u, w_ref[:, xr_lo:xr_lo + dr], preferred_element_type=F32)
        o_ref[lo:lo + mp, xr_lo:xr_lo + dr] = xr_p
        xr.append(xr_p)

    for lo_c, hi_c in ((0, xr_lo), (xr_lo + dr, din)):
        for lo in range(lo_c, hi_c, tn):
            hi = min(lo + tn, hi_c)
            o_ref[:, lo:hi] = jnp.dot(u_ref[...], w_ref[:, lo:hi], preferred_element_type=F32)

    h = list(_load_state(hc_ref, nb))
    for p in reversed(range(npieces)):
        lo = p * mp
        later = xcar_ref[...] if p == npieces - 1 else xr[p + 1][0:halo, :]
        xe = jnp.concatenate([xr[p], later], axis=0)
        xc = cb_ref[...] + cw_ref[RNN_CONV_W - 1:RNN_CONV_W, :] * xe[0:mp, :]
        for k in range(RNN_CONV_W - 1):
            off = (RNN_CONV_W - 1 - k) * r
            xc = xc + cw_ref[k:k + 1, :] * xe[off:off + mp, :]
        pre = [jnp.dot(xc[:, k * gw:(k + 1) * gw].astype(BF16), wg_ref[k], preferred_element_type=F32)
               for k in range(nblk)]
        for t in reversed(range(mp // r)):
            for sb in range(nb):
                lr = t * r + sb * SUBLANES
                rs = slice(lo + lr, lo + lr + SUBLANES)
                for g in range(ng):
                    cs = slice(g * LANES, (g + 1) * LANES)
                    k, off = divmod(g * LANES, gw)
                    hh = _rglru_unit(pre[k][lr:lr + SUBLANES, off:off + LANES],
                                     pre[k][lr:lr + SUBLANES, gw + off:gw + off + LANES],
                                     xc[lr:lr + SUBLANES, cs],
                                     par_ref[0:8, cs], par_ref[8:16, cs], par_ref[16:24, cs],
                                     h[sb * ng + g])
                    h[sb * ng + g] = hh
                    hb_ref[rs, cs] = hh

    xcar_ref[...] = xr[0][0:halo, :]
    _store_state(hc_ref, h, nb)
    hl_ref[...] = hc_ref[...]


def _projscan(x, view, sc, sh, g, w, rp, h0, r, tm, mp, tn, xr_lo):
    n, d = x.shape
    din = w.shape[1]
    dr = h0.shape[1]
    nt = n // tm
    const2 = lambda i: (0, 0)
    const3 = lambda i: (0, 0, 0)
    halo = (RNN_CONV_W - 1) * r
    return pl.pallas_call(
        functools.partial(_projscan_body, r=r, mp=mp, tn=tn, xr_lo=xr_lo),
        grid=(nt,),
        in_specs=[
            _x_spec(view, tm, order=lambda i: nt - 1 - i),
            pl.BlockSpec(sc.shape, const2),
            pl.BlockSpec(sh.shape, const2),
            pl.BlockSpec(g.shape, const2),
            pl.BlockSpec(w.shape, const2, pipeline_mode=pl.Buffered(1)),
            pl.BlockSpec(rp["cw"].shape, const2),
            pl.BlockSpec(rp["cb"].shape, const2),
            pl.BlockSpec(rp["wg"].shape, const3),
            pl.BlockSpec(rp["ba"].shape, const2),
            pl.BlockSpec(rp["bx"].shape, const2),
            pl.BlockSpec(rp["lam"].shape, const2),
            pl.BlockSpec(h0.shape, const2),
        ],
        out_specs=[
            pl.BlockSpec((tm, din), lambda i: (nt - 1 - i, 0)),
            pl.BlockSpec((tm, dr), lambda i: (nt - 1 - i, 0)),
            pl.BlockSpec(h0.shape, const2),
        ],
        out_shape=[
            jax.ShapeDtypeStruct((n, din), F32),
            jax.ShapeDtypeStruct((n, dr), F32),
            jax.ShapeDtypeStruct(h0.shape, F32),
        ],
        scratch_shapes=[
            pltpu.VMEM((tm, d), BF16),
            pltpu.VMEM((halo, dr), F32),
            pltpu.VMEM((3 * SUBLANES, dr), F32),
            pltpu.VMEM(h0.shape, F32),
        ],
        compiler_params=pltpu.CompilerParams(
            dimension_semantics=("arbitrary",),
            vmem_limit_bytes=56 * 1024 * 1024),
        name="proj_bwd_scan",
    )(_as_view(x, view), sc, sh, g, w, rp["cw"], rp["cb"], rp["wg"], rp["ba"], rp["bx"], rp["lam"], h0)


def _rows(ref, lo, n):
    if len(ref.shape) == 2:
        return ref[lo:lo + n, :]
    r = ref.shape[1]
    return ref[lo // r:(lo + n) // r, :, :].reshape(n, ref.shape[2])


def _set_rows(ref, lo, n, val):
    if len(ref.shape) == 2:
        ref[lo:lo + n, :] = val
    else:
        r = ref.shape[1]
        ref[lo // r:(lo + n) // r, :, :] = val.reshape(n // r, r, ref.shape[2])


def _rglru_unit(pre_r, pre_i, xc, ba, bx, c8, h):
    gate_r = _sigmoid(pre_r + ba)
    gate_i = _sigmoid(pre_i + bx)
    a = jnp.exp(c8 * gate_r)
    v = jnp.sqrt(1.0 - a * a) * (gate_i * xc)
    return a * h + v


def _mix_body(bg_ref, cg_ref, xa_ref, xr_ref, gr_ref, cgh_ref, xah_ref, hb_ref, x_ref,
              cw_ref, cb_ref, wg_ref, ba_ref, bx_ref, lam_ref, h0_ref,
              caw_ref, wout_ref, gn_ref, gate_ref,
              o_ref, hl_ref,
              xcar_ref, zcar_ref, par_ref, hc_ref, y_ref, *, r, mp):
    m, dr = xr_ref.shape
    dc = bg_ref.shape[1]
    nb = r // SUBLANES
    ng = dr // LANES
    nblk, gw, _ = wg_ref.shape
    halo = (RNN_CONV_W - 1) * r
    npieces = m // mp
    i = pl.program_id(0)
    last = pl.num_programs(0) - 1

    @pl.when(i == 0)
    def _init():
        xcar_ref[...] = jnp.zeros((halo, dr), F32)
        zcar_ref[...] = jnp.zeros((r, dc), F32)
        hc_ref[...] = h0_ref[...]
        _fill_rnn_params(par_ref, ba_ref, bx_ref, lam_ref)

    f32 = lambda ref, lo, n: ref[lo:lo + n, :].astype(F32)
    gate_rows = jnp.concatenate([gate_ref[...]] * (mp // MOD_ROWS), axis=0)
    h = list(_load_state(hc_ref, nb))
    for p in range(npieces):
        lo = p * mp
        if p == 0:
            xe = jnp.concatenate([xcar_ref[...], f32(xr_ref, 0, mp)], axis=0)
        else:
            xe = f32(xr_ref, lo - halo, mp + halo)
        xc = cb_ref[...] + cw_ref[0:1, :] * xe[0:mp, :]
        for k in range(1, RNN_CONV_W):
            xc = xc + cw_ref[k:k + 1, :] * xe[k * r:k * r + mp, :]
        pre = [jnp.dot(xc[:, k * gw:(k + 1) * gw].astype(BF16), wg_ref[k], preferred_element_type=F32)
               for k in range(nblk)]
        for t in range(mp // r):
            for sb in range(nb):
                lr = t * r + sb * SUBLANES
                rs = slice(lo + lr, lo + lr + SUBLANES)
                for g in range(ng):
                    cs = slice(g * LANES, (g + 1) * LANES)
                    k, off = divmod(g * LANES, gw)
                    hh = _rglru_unit(pre[k][lr:lr + SUBLANES, off:off + LANES],
                                     pre[k][lr:lr + SUBLANES, gw + off:gw + off + LANES],
                                     xc[lr:lr + SUBLANES, cs],
                                     par_ref[0:8, cs], par_ref[8:16, cs], par_ref[16:24, cs],
                                     h[sb * ng + g])
                    h[sb * ng + g] = hh
                    y_ref[rs, dc + g * LANES:dc + (g + 1) * LANES] = (
                        (hh + hb_ref[rs, cs]) * jax.nn.gelu(gr_ref[rs, cs].astype(F32)))

        z_prev = zcar_ref[...] if p == 0 else f32(cg_ref, lo - r, r) * f32(xa_ref, lo - r, r)
        if p == npieces - 1:
            z_next = jnp.where(i < last, f32(cgh_ref, 0, r) * f32(xah_ref, 0, r), 0.0)
        else:
            z_next = f32(cg_ref, lo + mp, r) * f32(xa_ref, lo + mp, r)
        ze = jnp.concatenate([z_prev, f32(cg_ref, lo, mp) * f32(xa_ref, lo, mp), z_next], axis=0)
        conv = (caw_ref[0:1, :] * ze[0:mp, :] + caw_ref[1:2, :] * ze[r:r + mp, :]
                + caw_ref[2:3, :] * ze[2 * r:2 * r + mp, :])
        y_ref[lo:lo + mp, 0:dc] = f32(bg_ref, lo, mp) * conv

        yo = jnp.dot(y_ref[lo:lo + mp, :].astype(BF16), wout_ref[...], preferred_element_type=F32)
        _set_rows(o_ref, lo, mp, _rows(x_ref, lo, mp) + gate_rows * _rms(yo, gn_ref[...]))

    xcar_ref[...] = f32(xr_ref, m - halo, halo)
    zcar_ref[...] = f32(cg_ref, m - r, r) * f32(xa_ref, m - r, r)
    _store_state(hc_ref, h, nb)
    hl_ref[...] = hc_ref[...]


def _mix(proj, hb, x, view, rp, h0, caw, wout, gn, gate, r, m, mp, cols):
    n, d = x.shape
    dr = h0.shape[1]
    dc = caw.shape[1]
    dmix = wout.shape[0]
    nt = n // m
    hr = r
    nh = n // hr
    const2 = lambda i: (0, 0)
    const3 = lambda i: (0, 0, 0)
    halo = (RNN_CONV_W - 1) * r
    col = lambda k: pl.BlockSpec((m, dc), lambda i: (i, k))
    nxt = lambda k: pl.BlockSpec((hr, dc), lambda i: (jnp.minimum((i + 1) * (m // hr), nh - 1), k))
    xspec = _x_spec(view, m)
    return pl.pallas_call(
        functools.partial(_mix_body, r=r, mp=mp),
        grid=(nt,),
        in_specs=[
            col(cols["bg"]), col(cols["cg"]), col(cols["xa"]), col(cols["xr"]), col(cols["gr"]),
            nxt(cols["cg"]), nxt(cols["xa"]),
            pl.BlockSpec((m, dr), lambda i: (i, 0)),
            xspec,
            pl.BlockSpec(rp["cw"].shape, const2),
            pl.BlockSpec(rp["cb"].shape, const2),
            pl.BlockSpec(rp["wg"].shape, const3),
            pl.BlockSpec(rp["ba"].shape, const2),
            pl.BlockSpec(rp["bx"].shape, const2),
            pl.BlockSpec(rp["lam"].shape, const2),
            pl.BlockSpec(h0.shape, const2),
            pl.BlockSpec(caw.shape, const2),
            pl.BlockSpec(wout.shape, const2, pipeline_mode=pl.Buffered(1)),
            pl.BlockSpec(gn.shape, const2),
            pl.BlockSpec(gate.shape, const2),
        ],
        out_specs=[
            xspec,
            pl.BlockSpec(h0.shape, const2),
        ],
        out_shape=[
            jax.ShapeDtypeStruct(_as_view(x, view).shape, F32),
            jax.ShapeDtypeStruct(h0.shape, F32),
        ],
        scratch_shapes=[
            pltpu.VMEM((halo, dr), F32),
            pltpu.VMEM((r, dc), F32),
            pltpu.VMEM((3 * SUBLANES, dr), F32),
            pltpu.VMEM(h0.shape, F32),
            pltpu.VMEM((m, dmix), F32),
        ],
        compiler_params=pltpu.CompilerParams(
            dimension_semantics=("arbitrary",),
            vmem_limit_bytes=56 * 1024 * 1024),
        name="mix_fwd",
    )(proj, proj, proj, proj, proj, proj, proj, hb, _as_view(x, view),
      rp["cw"], rp["cb"], rp["wg"], rp["ba"], rp["bx"], rp["lam"], h0, caw, wout, gn, gate)


def _ffn_body(x_ref, xp_ref, xn_ref, sc_ref, sh_ref, gate_ref, gin_ref, gout_ref,
              wg_ref, wv_ref, cg_ref, cv_ref, wd_ref, o_ref, u_ref, acc_ref, *, r, ch, ts, piece):
    tm, d = x_ref.shape
    tf = wd_ref.shape[0]
    i = pl.program_id(0)
    j = pl.program_id(1)
    ni = pl.num_programs(0)
    nj = pl.num_programs(1)
    hal = HALO_ROWS
    rows_all = tm + 2 * hal

    def conv(h, c_ref, cols):
        cw = c_ref[:, cols]
        return (cw[0:1, :] * h[hal - r:hal - r + tm, :] + cw[1:2, :] * h[hal:hal + tm, :]
                + cw[2:3, :] * h[hal + r:hal + r + tm, :])

    def sub_chunk(k, hg, hv):
        cols = slice(ts * k, ts * (k + 1))
        a = (jax.nn.gelu(conv(hg, cg_ref, cols)) * conv(hv, cv_ref, cols)).astype(BF16)
        return jnp.dot(a, wd_ref[cols, :], preferred_element_type=F32)

    def up(u, k):
        cols = slice(ts * k, ts * (k + 1))
        return (jnp.dot(u, wg_ref[:, cols], preferred_element_type=F32),
                jnp.dot(u, wv_ref[:, cols], preferred_element_type=F32))

    def chunk(first):
        u = u_ref[...]
        acc = first
        for k in range(0 if first is None else 1, tf // ts):
            p = sub_chunk(k, *up(u, k))
            acc = p if acc is None else acc + p
        return acc

    @pl.when(j == 0)
    def _first_chunk():
        src = lambda lo, n: (xp_ref[...] if lo < 0 else xn_ref[...] if lo >= tm else x_ref[lo:lo + n, :])
        hg, hv = [], []
        for lo in range(0, rows_all, piece):
            n = min(piece, rows_all - lo)
            parts = []
            for q in range(lo, lo + n, hal):
                xq = src(q - hal, hal)
                uq = (_rms(xq, gin_ref[...]) * (1.0 + sc_ref[0:hal, :]) + sh_ref[0:hal, :])
                if q == 0:
                    uq = jnp.where(i > 0, uq, 0.0)
                elif q == hal + tm:
                    uq = jnp.where(i < ni - 1, uq, 0.0)
                parts.append(uq.astype(BF16))
            u_p = jnp.concatenate(parts, axis=0)
            u_ref[lo:lo + n, :] = u_p
            g_p, v_p = up(u_p, 0)
            hg.append(g_p)
            hv.append(v_p)
        first = sub_chunk(0, jnp.concatenate(hg, axis=0), jnp.concatenate(hv, axis=0))
        acc_ref[...] = chunk(first)

    @pl.when(j > 0)
    def _later_chunk():
        acc_ref[...] += chunk(None)

    @pl.when(j == nj - 1)
    def _finish():
        def rows_chunk(c):
            rows = pl.ds(pl.multiple_of(c * ch, ch), ch)
            o_ref[rows, :] = x_ref[rows, :] + gate_ref[0:ch, :] * _rms(acc_ref[rows, :], gout_ref[...])

        _for_chunks(tm // ch, rows_chunk, unroll=NORM_UNROLL)


def _ffn(x, sc, sh, gate, gin, gout, wup, cw, wd, r, tm, tf, ts, ch, piece):
    n, d = x.shape
    dff = wd.shape[0]
    ni = n // tm
    nj = dff // tf
    nh = n // HALO_ROWS
    const = lambda i, j: (0, 0)
    return pl.pallas_call(
        functools.partial(_ffn_body, r=r, ch=ch, ts=ts, piece=piece),
        grid=(ni, nj),
        in_specs=[
            pl.BlockSpec((tm, d), lambda i, j: (i, 0)),
            pl.BlockSpec((HALO_ROWS, d), lambda i, j: (jnp.maximum(i * (tm // HALO_ROWS) - 1, 0), 0)),
            pl.BlockSpec((HALO_ROWS, d), lambda i, j: (jnp.minimum((i + 1) * (tm // HALO_ROWS), nh - 1), 0)),
            pl.BlockSpec(sc.shape, const),
            pl.BlockSpec(sh.shape, const),
            pl.BlockSpec(gate.shape, const),
            pl.BlockSpec(gin.shape, const),
            pl.BlockSpec(gout.shape, const),
            pl.BlockSpec((d, tf), lambda i, j: (0, j)),
            pl.BlockSpec((d, tf), lambda i, j: (0, nj + j)),
            pl.BlockSpec((FFN_CONV_W, tf), lambda i, j: (0, j)),
            pl.BlockSpec((FFN_CONV_W, tf), lambda i, j: (0, nj + j)),
            pl.BlockSpec((tf, d), lambda i, j: (j, 0)),
        ],
        out_specs=pl.BlockSpec((tm, d), lambda i, j: (i, 0)),
        out_shape=jax.ShapeDtypeStruct((n, d), F32),
        scratch_shapes=[
            pltpu.VMEM((tm + 2 * HALO_ROWS, d), BF16),
            pltpu.VMEM((tm, d), F32),
        ],
        compiler_params=pltpu.CompilerParams(
            dimension_semantics=("arbitrary", "arbitrary"),
            vmem_limit_bytes=56 * 1024 * 1024),
        name="conv_ffn",
    )(x, x, x, sc, sh, gate, gin, gout, wup, wup, cw, cw, wd)


def _block_diag(w, gw):
    heads, hd, _ = w.shape
    hpb = gw // hd
    w = w.reshape(heads // hpb, hpb, hd, hd)
    eye = jnp.eye(hpb, dtype=w.dtype)
    return jnp.einsum("nhij,hg->nhigj", w, eye).reshape(heads // hpb, gw, gw)


def _plan(d, dr, dff, n_rows, r, grid_rows):
    seq_rows = grid_rows * r if grid_rows else n_rows
    return dict(
        proj_rows=min(256, seq_rows),
        mix_rows=min(256, seq_rows),
        ffn_rows=min(512, n_rows),
        ffn_cols=min(1024, dff),
        norm_piece=min(128, n_rows),
        wide_chunk=2 * SUBLANES,
        mix_chunk=4 * SUBLANES,
        mix_piece=min(128, seq_rows),
        ffn_sub=min(MXU_WIDTH, dff),
        proj_cols=min(1024, 3 * (d - dr) + 2 * dr),
    )


def _tile_rows(mod, r, rows):
    return jnp.tile(mod, (rows // r, 1))


def _layer(x, r, grid_rows, mods, h0, lw, plan):
    n, d = x.shape
    sh1, sc1, g1, sh2, sc2, g2 = [_tile_rows(m, r, MOD_ROWS) for m in jnp.split(mods, N_MOD, axis=-1)]
    if grid_rows:
        view = dict(kind="grid", r=r, d=d, grid_rows=grid_rows)
    else:
        view = dict(kind="flat", d=d)
    ng = lw["norm_g"]
    dr = h0.shape[-1]
    proj, hb, last_b = _projscan(x, view, sc1, sh1, ng[0:1], lw["w_in"], lw["rnn"][1], h0[:, 1], r,
                                 plan["proj_rows"], plan["mix_piece"], plan["proj_cols"], lw["cols"]["xr"] * dr)
    x, last_f = _mix(proj, hb, x, view, lw["rnn"][0], h0[:, 0], lw["conv_a_w"], lw["w_out"],
                     ng[1:2], g1, r, plan["mix_rows"], plan["mix_piece"], lw["cols"])
    x = x.reshape(n, d)
    x = _ffn(x, sc2, sh2, g2, ng[2:3], ng[3:4], lw["ffn_up"], lw["ffn_conv_w"], lw["ffn_down"],
             r, plan["ffn_rows"], plan["ffn_cols"], plan["ffn_sub"], plan["wide_chunk"], plan["norm_piece"])
    return x, jnp.stack([last_f, last_b], axis=1)


def _forward(x_prompt, x_sample, state_h, c, c_ctx, w_ada, b_ada, norm_g, w_in, conv_a_w,
             rnn_conv_w, rnn_conv_b, rnn_w_a, rnn_b_a, rnn_w_x, rnn_b_x, rnn_lam, w_out,
             ffn_up, ffn_conv_w, ffn_down, plan_overrides=None):
    b, s, d = x_prompt.shape
    bd, l, _ = x_sample.shape
    depth = w_ada.shape[0]
    dr = rnn_lam.shape[-1]
    dc = conv_a_w.shape[-1]
    dff = ffn_down.shape[1]
    grid_rows = l // GRID_W
    gw = min(MXU_WIDTH, dr)

    plan_p = _plan(d, dr, dff, s * b, b, 0)
    plan_s = _plan(d, dr, dff, l * bd, bd, grid_rows)
    for p in (plan_p, plan_s):
        p.update(plan_overrides or {})

    xp = jnp.transpose(x_prompt, (1, 0, 2)).reshape(s * b, d)
    xs = jnp.transpose(x_sample, (1, 0, 2)).reshape(l * bd, d)

    cc = jnp.concatenate([c, jnp.broadcast_to(c_ctx[None, :], (SUBLANES, d))], axis=0)
    mods = _modulation(cc, w_ada, b_ada, min(1024, N_MOD * d))

    h_zero = jnp.zeros((b, 2, dr), F32)
    states = []
    for li in range(depth):
        rnn = []
        for di in range(2):
            wg = jnp.concatenate([_block_diag(rnn_w_a[li, di], gw), _block_diag(rnn_w_x[li, di], gw)],
                                 axis=-1).astype(BF16)
            rnn.append(dict(cw=rnn_conv_w[li, di], cb=rnn_conv_b[li, di][None, :], wg=wg,
                            ba=rnn_b_a[li, di][None, :], bx=rnn_b_x[li, di][None, :],
                            lam=rnn_lam[li, di][None, :], xr_col=3 * dc // dr))
        lw = dict(
            norm_g=norm_g[li],
            w_in=w_in[li].astype(BF16),
            conv_a_w=conv_a_w[li],
            rnn=rnn,
            w_out=w_out[li].astype(BF16),
            ffn_up=ffn_up[li].astype(BF16),
            ffn_conv_w=ffn_conv_w[li],
            ffn_down=ffn_down[li].astype(BF16),
            cols=dict(bg=0, cg=1, xa=2, xr=3 * dc // dr, gr=3 * dc // dr + 1),
        )
        mod_ctx = jnp.broadcast_to(mods[li, bd:bd + 1], (b, N_MOD * d))
        xp, st = _layer(xp, b, 0, mod_ctx, h_zero, lw, plan_p)
        states.append(st)
        xs, _ = _layer(xs, bd, grid_rows if li % 2 == 1 else 0, mods[li, :bd], state_h[:, li], lw, plan_s)

    y_prompt = jnp.transpose(xp.reshape(s, b, d), (1, 0, 2))
    y_sample = jnp.transpose(xs.reshape(l, bd, d), (1, 0, 2))
    return y_prompt, y_sample, jnp.stack(states, axis=1)


def kernel(x_prompt, x_sample, state_h, c, c_ctx, w_ada, b_ada, norm_g, w_in, conv_a_w, rnn_conv_w,
           rnn_conv_b, rnn_w_a, rnn_b_a, rnn_w_x, rnn_b_x, rnn_lam, w_out, ffn_up, ffn_conv_w, ffn_down):
    return _forward(x_prompt, x_sample, state_h, c, c_ctx, w_ada, b_ada, norm_g, w_in, conv_a_w,
                    rnn_conv_w, rnn_conv_b, rnn_w_a, rnn_b_a, rnn_w_x, rnn_b_x, rnn_lam, w_out,
                    ffn_up, ffn_conv_w, ffn_down)
```

```python
import functools

import jax
import jax.numpy as jnp
from jax import lax
from jax.experimental import pallas as pl
from jax.experimental.pallas import tpu as pltpu

GRID_W = 64
N_RNN_HEADS = 16
RGLRU_C = 8.0
N_MOD = 6
EPS = 1e-6
RNN_CONV_W = 4
CONV_A_W = 3
FFN_CONV_W = 3

SUBLANES = 8
LANES = 128
MXU_WIDTH = 256
VMEM_BYTES = 64 * 1024 * 1024
HALO_ROWS = 16
MOD_ROWS = 32

F32 = jnp.float32
BF16 = jnp.bfloat16


def _rms(x, g):
    var = jnp.mean(x * x, axis=-1, keepdims=True)
    return x * lax.rsqrt(var + EPS) * g


def _sigmoid(x):
    return 0.5 * jnp.tanh(0.5 * x) + 0.5


def _mod_body(cc_ref, w_ref, b_ref, o_ref):
    s = cc_ref[...]
    s = s * jax.nn.sigmoid(s)
    o_ref[...] = jnp.dot(s.astype(BF16), w_ref[...].astype(BF16),
                         preferred_element_type=F32) + b_ref[...]


def _modulation(cc, w_ada, b_ada, tn):
    depth, d, n = w_ada.shape
    rows = cc.shape[0]
    return pl.pallas_call(
        _mod_body,
        grid=(depth, n // tn),
        in_specs=[
            pl.BlockSpec((rows, d), lambda l, j: (0, 0)),
            pl.BlockSpec((None, d, tn), lambda l, j: (l, 0, j)),
            pl.BlockSpec((None, 1, tn), lambda l, j: (l, 0, j)),
        ],
        out_specs=pl.BlockSpec((None, rows, tn), lambda l, j: (l, 0, j)),
        out_shape=jax.ShapeDtypeStruct((depth, rows, n), F32),
        compiler_params=pltpu.CompilerParams(
            dimension_semantics=("arbitrary", "arbitrary"),
            vmem_limit_bytes=40 * 1024 * 1024),
        name="adaln_modulation",
    )(cc, w_ada, b_ada.reshape(depth, 1, n))


def _x_spec(view, tile_rows, order=lambda i: i):
    if view["kind"] == "flat":
        return pl.BlockSpec((tile_rows, view["d"]), lambda i: (order(i), 0))
    r = view["r"]
    steps = tile_rows // r
    tpc = view["grid_rows"] // steps
    return pl.BlockSpec((steps, None, r, view["d"]), lambda i: (order(i) % tpc, order(i) // tpc, 0, 0))


def _as_view(x, view):
    if view["kind"] == "flat":
        return x
    return x.reshape(view["grid_rows"], GRID_W, view["r"], view["d"])


def _fill_rnn_params(par_ref, ba_ref, bx_ref, lam_ref):
    dr = par_ref.shape[1]
    par_ref[0:8, :] = jnp.broadcast_to(ba_ref[...], (SUBLANES, dr))
    par_ref[8:16, :] = jnp.broadcast_to(bx_ref[...], (SUBLANES, dr))
    par_ref[16:24, :] = jnp.broadcast_to(-RGLRU_C * jax.nn.softplus(-lam_ref[...]), (SUBLANES, dr))


def _load_state(hc_ref, nb):
    ng = hc_ref.shape[1] // LANES
    return tuple(hc_ref[sb * SUBLANES:(sb + 1) * SUBLANES, g * LANES:(g + 1) * LANES]
                 for sb in range(nb) for g in range(ng))


def _store_state(hc_ref, h, nb):
    ng = hc_ref.shape[1] // LANES
    for sb in range(nb):
        for g in range(ng):
            hc_ref[sb * SUBLANES:(sb + 1) * SUBLANES, g * LANES:(g + 1) * LANES] = h[sb * ng + g]


def _projscan_body(x_ref, sc_ref, sh_ref, g_ref, w_ref, cw_ref, cb_ref, wg_ref, ba_ref, bx_ref, lam_ref, h0_ref,
                   o_ref, hb_ref, hl_ref,
                   u_ref, xcar_ref, par_ref, hc_ref, *, r, mp, tn, xr_lo):
    tm, d = u_ref.shape
    dr = hb_ref.shape[1]
    din = o_ref.shape[1]
    nb = r // SUBLANES
    ng = dr // LANES
    nblk, gw, _ = wg_ref.shape
    halo = (RNN_CONV_W - 1) * r
    npieces = tm // mp

    @pl.when(pl.program_id(0) == 0)
    def _init():
        xcar_ref[...] = jnp.zeros((halo, dr), F32)
        hc_ref[...] = h0_ref[...]
        _fill_rnn_params(par_ref, ba_ref, bx_ref, lam_ref)

    sc = jnp.concatenate([sc_ref[...]] * (mp // MOD_ROWS), axis=0)
    sh = jnp.concatenate([sh_ref[...]] * (mp // MOD_ROWS), axis=0)
    xr = []
    for p in range(npieces):
        lo = p * mp
        u = (_rms(_rows(x_ref, lo, mp), g_ref[...]) * (1.0 + sc) + sh).astype(BF16)
        u_ref[lo:lo + mp, :] = u
        xr_p = jnp.dot(u, w_ref[:, xr_lo:xr_lo + dr], preferred_element_type=F32)
        o_ref[lo:lo + mp, xr_lo:xr_lo + dr] = xr_p
        xr.append(xr_p)

    for lo_c, hi_c in ((0, xr_lo), (xr_lo + dr, din)):
        for lo in range(lo_c, hi_c, tn):
            hi = min(lo + tn, hi_c)
            o_ref[:, lo:hi] = jnp.dot(u_ref[...], w_ref[:, lo:hi], preferred_element_type=F32)

    h = list(_load_state(hc_ref, nb))
    for p in reversed(range(npieces)):
        lo = p * mp
        later = xcar_ref[...] if p == npieces - 1 else xr[p + 1][0:halo, :]
        xe = jnp.concatenate([xr[p], later], axis=0)
        xc = cb_ref[...] + cw_ref[RNN_CONV_W - 1:RNN_CONV_W, :] * xe[0:mp, :]
        for k in range(RNN_CONV_W - 1):
            off = (RNN_CONV_W - 1 - k) * r
            xc = xc + cw_ref[k:k + 1, :] * xe[off:off + mp, :]
        pre = [jnp.dot(xc[:, k * gw:(k + 1) * gw].astype(BF16), wg_ref[k], preferred_element_type=F32)
               for k in range(nblk)]
        for t in reversed(range(mp // r)):
            for sb in range(nb):
                lr = t * r + sb * SUBLANES
                rs = slice(lo + lr, lo + lr + SUBLANES)
                for g in range(ng):
                    cs = slice(g * LANES, (g + 1) * LANES)
                    k, off = divmod(g * LANES, gw)
                    hh = _rglru_unit(pre[k][lr:lr + SUBLANES, off:off + LANES],
                                     pre[k][lr:lr + SUBLANES, gw + off:gw + off + LANES],
                                     xc[lr:lr + SUBLANES, cs],
                                     par_ref[0:8, cs], par_ref[8:16, cs], par_ref[16:24, cs],
                                     h[sb * ng + g])
                    h[sb * ng + g] = hh
                    hb_ref[rs, cs] = hh

    xcar_ref[...] = xr[0][0:halo, :]
    _store_state(hc_ref, h, nb)
    hl_ref[...] = hc_ref[...]


def _projscan(x, view, sc, sh, g, w, rp, h0, r, tm, mp, tn, xr_lo):
    n, d = x.shape
    din = w.shape[1]
    dr = h0.shape[1]
    nt = n // tm
    const2 = lambda i: (0, 0)
    const3 = lambda i: (0, 0, 0)
    halo = (RNN_CONV_W - 1) * r
    return pl.pallas_call(
        functools.partial(_projscan_body, r=r, mp=mp, tn=tn, xr_lo=xr_lo),
        grid=(nt,),
        in_specs=[
            _x_spec(view, tm, order=lambda i: nt - 1 - i),
            pl.BlockSpec(sc.shape, const2),
            pl.BlockSpec(sh.shape, const2),
            pl.BlockSpec(g.shape, const2),
            pl.BlockSpec(w.shape, const2, pipeline_mode=pl.Buffered(1)),
            pl.BlockSpec(rp["cw"].shape, const2),
            pl.BlockSpec(rp["cb"].shape, const2),
            pl.BlockSpec(rp["wg"].shape, const3),
            pl.BlockSpec(rp["ba"].shape, const2),
            pl.BlockSpec(rp["bx"].shape, const2),
            pl.BlockSpec(rp["lam"].shape, const2),
            pl.BlockSpec(h0.shape, const2),
        ],
        out_specs=[
            pl.BlockSpec((tm, din), lambda i: (nt - 1 - i, 0)),
            pl.BlockSpec((tm, dr), lambda i: (nt - 1 - i, 0)),
            pl.BlockSpec(h0.shape, const2),
        ],
        out_shape=[
            jax.ShapeDtypeStruct((n, din), F32),
            jax.ShapeDtypeStruct((n, dr), F32),
            jax.ShapeDtypeStruct(h0.shape, F32),
        ],
        scratch_shapes=[
            pltpu.VMEM((tm, d), BF16),
            pltpu.VMEM((halo, dr), F32),
            pltpu.VMEM((3 * SUBLANES, dr), F32),
            pltpu.VMEM(h0.shape, F32),
        ],
        compiler_params=pltpu.CompilerParams(
            dimension_semantics=("arbitrary",),
            vmem_limit_bytes=56 * 1024 * 1024),
        name="proj_bwd_scan",
    )(_as_view(x, view), sc, sh, g, w, rp["cw"], rp["cb"], rp["wg"], rp["ba"], rp["bx"], rp["lam"], h0)


def _rows(ref, lo, n):
    if len(ref.shape) == 2:
        return ref[lo:lo + n, :]
    r = ref.shape[1]
    return ref[lo // r:(lo + n) // r, :, :].reshape(n, ref.shape[2])


def _set_rows(ref, lo, n, val):
    if len(ref.shape) == 2:
        ref[lo:lo + n, :] = val
    else:
        r = ref.shape[1]
        ref[lo // r:(lo + n) // r, :, :] = val.reshape(n // r, r, ref.shape[2])


def _rglru_unit(pre_r, pre_i, xc, ba, bx, c8, h):
    gate_r = _sigmoid(pre_r + ba)
    gate_i = _sigmoid(pre_i + bx)
    a = jnp.exp(c8 * gate_r)
    v = jnp.sqrt(1.0 - a * a) * (gate_i * xc)
    return a * h + v


def _mix_body(bg_ref, cg_ref, xa_ref, xr_ref, gr_ref, cgh_ref, xah_ref, hb_ref, x_ref,
              cw_ref, cb_ref, wg_ref, ba_ref, bx_ref, lam_ref, h0_ref,
              caw_ref, wout_ref, gn_ref, gate_ref,
              o_ref, hl_ref,
              xcar_ref, zcar_ref, par_ref, hc_ref, y_ref, *, r, mp):
    m, dr = xr_ref.shape
    dc = bg_ref.shape[1]
    nb = r // SUBLANES
    ng = dr // LANES
    nblk, gw, _ = wg_ref.shape
    halo = (RNN_CONV_W - 1) * r
    npieces = m // mp
    i = pl.program_id(0)
    last = pl.num_programs(0) - 1

    @pl.when(i == 0)
    def _init():
        xcar_ref[...] = jnp.zeros((halo, dr), F32)
        zcar_ref[...] = jnp.zeros((r, dc), F32)
        hc_ref[...] = h0_ref[...]
        _fill_rnn_params(par_ref, ba_ref, bx_ref, lam_ref)

    f32 = lambda ref, lo, n: ref[lo:lo + n, :].astype(F32)
    gate_rows = jnp.concatenate([gate_ref[...]] * (mp // MOD_ROWS), axis=0)
    h = list(_load_state(hc_ref, nb))
    for p in range(npieces):
        lo = p * mp
        if p == 0:
            xe = jnp.concatenate([xcar_ref[...], f32(xr_ref, 0, mp)], axis=0)
        else:
            xe = f32(xr_ref, lo - halo, mp + halo)
        xc = cb_ref[...] + cw_ref[0:1, :] * xe[0:mp, :]
        for k in range(1, RNN_CONV_W):
            xc = xc + cw_ref[k:k + 1, :] * xe[k * r:k * r + mp, :]
        pre = [jnp.dot(xc[:, k * gw:(k + 1) * gw].astype(BF16), wg_ref[k], preferred_element_type=F32)
               for k in range(nblk)]
        for t in range(mp // r):
            for sb in range(nb):
                lr = t * r + sb * SUBLANES
                rs = slice(lo + lr, lo + lr + SUBLANES)
                for g in range(ng):
                    cs = slice(g * LANES, (g + 1) * LANES)
                    k, off = divmod(g * LANES, gw)
                    hh = _rglru_unit(pre[k][lr:lr + SUBLANES, off:off + LANES],
                                     pre[k][lr:lr + SUBLANES, gw + off:gw + off + LANES],
                                     xc[lr:lr + SUBLANES, cs],
                                     par_ref[0:8, cs], par_ref[8:16, cs], par_ref[16:24, cs],
                                     h[sb * ng + g])
                    h[sb * ng + g] = hh
                    y_ref[rs, dc + g * LANES:dc + (g + 1) * LANES] = (
                        (hh + hb_ref[rs, cs]) * jax.nn.gelu(gr_ref[rs, cs].astype(F32)))

        z_prev = zcar_ref[...] if p == 0 else f32(cg_ref, lo - r, r) * f32(xa_ref, lo - r, r)
        if p == npieces - 1:
            z_next = jnp.where(i < last, f32(cgh_ref, 0, r) * f32(xah_ref, 0, r), 0.0)
        else:
            z_next = f32(cg_ref, lo + mp, r) * f32(xa_ref, lo + mp, r)
        ze = jnp.concatenate([z_prev, f32(cg_ref, lo, mp) * f32(xa_ref, lo, mp), z_next], axis=0)
        conv = (caw_ref[0:1, :] * ze[0:mp, :] + caw_ref[1:2, :] * ze[r:r + mp, :]
                + caw_ref[2:3, :] * ze[2 * r:2 * r + mp, :])
        y_ref[lo:lo + mp, 0:dc] = f32(bg_ref, lo, mp) * conv

        yo = jnp.dot(y_ref[lo:lo + mp, :].astype(BF16), wout_ref[...], preferred_element_type=F32)
        _set_rows(o_ref, lo, mp, _rows(x_ref, lo, mp) + gate_rows * _rms(yo, gn_ref[...]))

    xcar_ref[...] = f32(xr_ref, m - halo, halo)
    zcar_ref[...] = f32(cg_ref, m - r, r) * f32(xa_ref, m - r, r)
    _store_state(hc_ref, h, nb)
    hl_ref[...] = hc_ref[...]


def _mix(proj, hb, x, view, rp, h0, caw, wout, gn, gate, r, m, mp, cols):
    n, d = x.shape
    dr = h0.shape[1]
    dc = caw.shape[1]
    dmix = wout.shape[0]
    nt = n // m
    hr = r
    nh = n // hr
    const2 = lambda i: (0, 0)
    const3 = lambda i: (0, 0, 0)
    halo = (RNN_CONV_W - 1) * r
    col = lambda k: pl.BlockSpec((m, dc), lambda i: (i, k))
    nxt = lambda k: pl.BlockSpec((hr, dc), lambda i: (jnp.minimum((i + 1) * (m // hr), nh - 1), k))
    xspec = _x_spec(view, m)
    return pl.pallas_call(
        functools.partial(_mix_body, r=r, mp=mp),
        grid=(nt,),
        in_specs=[
            col(cols["bg"]), col(cols["cg"]), col(cols["xa"]), col(cols["xr"]), col(cols["gr"]),
            nxt(cols["cg"]), nxt(cols["xa"]),
            pl.BlockSpec((m, dr), lambda i: (i, 0)),
            xspec,
            pl.BlockSpec(rp["cw"].shape, const2),
            pl.BlockSpec(rp["cb"].shape, const2),
            pl.BlockSpec(rp["wg"].shape, const3),
            pl.BlockSpec(rp["ba"].shape, const2),
            pl.BlockSpec(rp["bx"].shape, const2),
            pl.BlockSpec(rp["lam"].shape, const2),
            pl.BlockSpec(h0.shape, const2),
            pl.BlockSpec(caw.shape, const2),
            pl.BlockSpec(wout.shape, const2, pipeline_mode=pl.Buffered(1)),
            pl.BlockSpec(gn.shape, const2),
            pl.BlockSpec(gate.shape, const2),
        ],
        out_specs=[
            xspec,
            pl.BlockSpec(h0.shape, const2),
        ],
        out_shape=[
            jax.ShapeDtypeStruct(_as_view(x, view).shape, F32),
            jax.ShapeDtypeStruct(h0.shape, F32),
        ],
        scratch_shapes=[
            pltpu.VMEM((halo, dr), F32),
            pltpu.VMEM((r, dc), F32),
            pltpu.VMEM((3 * SUBLANES, dr), F32),
            pltpu.VMEM(h0.shape, F32),
            pltpu.VMEM((m, dmix), F32),
        ],
        compiler_params=pltpu.CompilerParams(
            dimension_semantics=("arbitrary",),
            vmem_limit_bytes=56 * 1024 * 1024),
        name="mix_fwd",
    )(proj, proj, proj, proj, proj, proj, proj, hb, _as_view(x, view),
      rp["cw"], rp["cb"], rp["wg"], rp["ba"], rp["bx"], rp["lam"], h0, caw, wout, gn, gate)


def _ffn_body(x_ref, xp_ref, xn_ref, sc_ref, sh_ref, gate_ref, gin_ref, gout_ref,
              wg_ref, wv_ref, cg_ref, cv_ref, wd_ref, o_ref, u_ref, acc_ref, *, r, ts, piece):
    tm, d = x_ref.shape
    tf = wd_ref.shape[0]
    i = pl.program_id(0)
    j = pl.program_id(1)
    ni = pl.num_programs(0)
    nj = pl.num_programs(1)
    hal = HALO_ROWS
    rows_all = tm + 2 * hal

    def conv(h, c_ref, cols):
        cw = c_ref[:, cols]
        return (cw[0:1, :] * h[hal - r:hal - r + tm, :] + cw[1:2, :] * h[hal:hal + tm, :]
                + cw[2:3, :] * h[hal + r:hal + r + tm, :])

    def sub_chunk(k, hg, hv):
        cols = slice(ts * k, ts * (k + 1))
        a = (jax.nn.gelu(conv(hg, cg_ref, cols)) * conv(hv, cv_ref, cols)).astype(BF16)
        return jnp.dot(a, wd_ref[cols, :], preferred_element_type=F32)

    def up(u, k):
        cols = slice(ts * k, ts * (k + 1))
        return (jnp.dot(u, wg_ref[:, cols], preferred_element_type=F32),
                jnp.dot(u, wv_ref[:, cols], preferred_element_type=F32))

    def chunk(first):
        u = u_ref[...]
        acc = first
        for k in range(0 if first is None else 1, tf // ts):
            p = sub_chunk(k, *up(u, k))
            acc = p if acc is None else acc + p
        return acc

    @pl.when(j == 0)
    def _first_chunk():
        src = lambda lo, n: (xp_ref[...] if lo < 0 else xn_ref[...] if lo >= tm else x_ref[lo:lo + n, :])
        hg, hv = [], []
        for lo in range(0, rows_all, piece):
            n = min(piece, rows_all - lo)
            parts = []
            for q in range(lo, lo + n, hal):
                xq = src(q - hal, hal)
                uq = (_rms(xq, gin_ref[...]) * (1.0 + sc_ref[0:hal, :]) + sh_ref[0:hal, :])
                if q == 0:
                    uq = jnp.where(i > 0, uq, 0.0)
                elif q == hal + tm:
                    uq = jnp.where(i < ni - 1, uq, 0.0)
                parts.append(uq.astype(BF16))
            u_p = jnp.concatenate(parts, axis=0)
            u_ref[lo:lo + n, :] = u_p
            g_p, v_p = up(u_p, 0)
            hg.append(g_p)
            hv.append(v_p)
        first = sub_chunk(0, jnp.concatenate(hg, axis=0), jnp.concatenate(hv, axis=0))
        acc_ref[...] = chunk(first)

    @pl.when(jnp.logical_and(j > 0, j < nj - 1))
    def _middle_chunk():
        acc_ref[...] += chunk(None)

    @pl.when(jnp.logical_and(j > 0, j == nj - 1))
    def _last_chunk():
        total = acc_ref[...] + chunk(None)
        gate = jnp.concatenate([gate_ref[...]] * (piece // MOD_ROWS), axis=0)
        for lo in range(0, tm, piece):
            rows = slice(lo, lo + piece)
            o_ref[rows, :] = x_ref[rows, :] + gate * _rms(total[rows, :], gout_ref[...])


def _ffn(x, sc, sh, gate, gin, gout, wup, cw, wd, r, tm, tf, ts, piece):
    n, d = x.shape
    dff = wd.shape[0]
    assert n % tm == 0 and dff % tf == 0 and tf % ts == 0 and tm % HALO_ROWS == 0
    assert dff // tf >= 2, "the first and last hidden chunk of a tile must be different grid steps"
    ni = n // tm
    nj = dff // tf
    nh = n // HALO_ROWS
    const = lambda i, j: (0, 0)
    return pl.pallas_call(
        functools.partial(_ffn_body, r=r, ts=ts, piece=piece),
        grid=(ni, nj),
        in_specs=[
            pl.BlockSpec((tm, d), lambda i, j: (i, 0)),
            pl.BlockSpec((HALO_ROWS, d), lambda i, j: (jnp.maximum(i * (tm // HALO_ROWS) - 1, 0), 0)),
            pl.BlockSpec((HALO_ROWS, d), lambda i, j: (jnp.minimum((i + 1) * (tm // HALO_ROWS), nh - 1), 0)),
            pl.BlockSpec(sc.shape, const),
            pl.BlockSpec(sh.shape, const),
            pl.BlockSpec(gate.shape, const),
            pl.BlockSpec(gin.shape, const),
            pl.BlockSpec(gout.shape, const),
            pl.BlockSpec((d, tf), lambda i, j: (0, j)),
            pl.BlockSpec((d, tf), lambda i, j: (0, nj + j)),
            pl.BlockSpec((FFN_CONV_W, tf), lambda i, j: (0, j)),
            pl.BlockSpec((FFN_CONV_W, tf), lambda i, j: (0, nj + j)),
            pl.BlockSpec((tf, d), lambda i, j: (j, 0)),
        ],
        out_specs=pl.BlockSpec((tm, d), lambda i, j: (i, 0)),
        out_shape=jax.ShapeDtypeStruct((n, d), F32),
        scratch_shapes=[
            pltpu.VMEM((tm + 2 * HALO_ROWS, d), BF16),
            pltpu.VMEM((tm, d), F32),
        ],
        compiler_params=pltpu.CompilerParams(
            dimension_semantics=("arbitrary", "arbitrary"),
            vmem_limit_bytes=56 * 1024 * 1024),
        name="conv_ffn",
    )(x, x, x, sc, sh, gate, gin, gout, wup, wup, cw, cw, wd)


def _block_diag(w, gw):
    heads, hd, _ = w.shape
    hpb = gw // hd
    w = w.reshape(heads // hpb, hpb, hd, hd)
    eye = jnp.eye(hpb, dtype=w.dtype)
    return jnp.einsum("nhij,hg->nhigj", w, eye).reshape(heads // hpb, gw, gw)


def _plan(d, dr, dff, n_rows, r, grid_rows):
    seq_rows = grid_rows * r if grid_rows else n_rows
    return dict(
        proj_rows=min(256, seq_rows),
        mix_rows=min(256, seq_rows),
        ffn_rows=min(512, n_rows),
        ffn_cols=min(1024, dff),
        norm_piece=min(128, n_rows),
        mix_piece=min(128, seq_rows),
        ffn_sub=min(2 * MXU_WIDTH, dff),
        proj_cols=3 * (d - dr),
    )


def _tile_rows(mod, r, rows):
    return jnp.tile(mod, (rows // r, 1))


def _layer(x, r, grid_rows, mods, h0, lw, plan):
    n, d = x.shape
    sh1, sc1, g1, sh2, sc2, g2 = [_tile_rows(m, r, MOD_ROWS) for m in jnp.split(mods, N_MOD, axis=-1)]
    if grid_rows:
        view = dict(kind="grid", r=r, d=d, grid_rows=grid_rows)
    else:
        view = dict(kind="flat", d=d)
    ng = lw["norm_g"]
    dr = h0.shape[-1]
    proj, hb, last_b = _projscan(x, view, sc1, sh1, ng[0:1], lw["w_in"], lw["rnn"][1], h0[:, 1], r,
                                 plan["proj_rows"], plan["mix_piece"], plan["proj_cols"], lw["cols"]["xr"] * dr)
    x, last_f = _mix(proj, hb, x, view, lw["rnn"][0], h0[:, 0], lw["conv_a_w"], lw["w_out"],
                     ng[1:2], g1, r, plan["mix_rows"], plan["mix_piece"], lw["cols"])
    x = x.reshape(n, d)
    x = _ffn(x, sc2, sh2, g2, ng[2:3], ng[3:4], lw["ffn_up"], lw["ffn_conv_w"], lw["ffn_down"],
             r, plan["ffn_rows"], plan["ffn_cols"], plan["ffn_sub"], plan["norm_piece"])
    return x, jnp.stack([last_f, last_b], axis=1)


def _forward(x_prompt, x_sample, state_h, c, c_ctx, w_ada, b_ada, norm_g, w_in, conv_a_w,
             rnn_conv_w, rnn_conv_b, rnn_w_a, rnn_b_a, rnn_w_x, rnn_b_x, rnn_lam, w_out,
             ffn_up, ffn_conv_w, ffn_down, plan_overrides=None):
    b, s, d = x_prompt.shape
    bd, l, _ = x_sample.shape
    depth = w_ada.shape[0]
    dr = rnn_lam.shape[-1]
    dc = conv_a_w.shape[-1]
    dff = ffn_down.shape[1]
    grid_rows = l // GRID_W
    gw = min(MXU_WIDTH, dr)

    plan_p = _plan(d, dr, dff, s * b, b, 0)
    plan_s = _plan(d, dr, dff, l * bd, bd, grid_rows)
    for p in (plan_p, plan_s):
        p.update(plan_overrides or {})

    xp = jnp.transpose(x_prompt, (1, 0, 2)).reshape(s * b, d)
    xs = jnp.transpose(x_sample, (1, 0, 2)).reshape(l * bd, d)

    cc = jnp.concatenate([c, jnp.broadcast_to(c_ctx[None, :], (SUBLANES, d))], axis=0)
    mods = _modulation(cc, w_ada, b_ada, min(1024, N_MOD * d))

    h_zero = jnp.zeros((b, 2, dr), F32)
    states = []
    for li in range(depth):
        rnn = []
        for di in range(2):
            wg = jnp.concatenate([_block_diag(rnn_w_a[li, di], gw), _block_diag(rnn_w_x[li, di], gw)],
                                 axis=-1).astype(BF16)
            rnn.append(dict(cw=rnn_conv_w[li, di], cb=rnn_conv_b[li, di][None, :], wg=wg,
                            ba=rnn_b_a[li, di][None, :], bx=rnn_b_x[li, di][None, :],
                            lam=rnn_lam[li, di][None, :], xr_col=3 * dc // dr))
        lw = dict(
            norm_g=norm_g[li],
            w_in=w_in[li].astype(BF16),
            conv_a_w=conv_a_w[li],
            rnn=rnn,
            w_out=w_out[li].astype(BF16),
            ffn_up=ffn_up[li].astype(BF16),
            ffn_conv_w=ffn_conv_w[li],
            ffn_down=ffn_down[li].astype(BF16),
            cols=dict(bg=0, cg=1, xa=2, xr=3 * dc // dr, gr=3 * dc // dr + 1),
        )
        mod_ctx = jnp.broadcast_to(mods[li, bd:bd + 1], (b, N_MOD * d))
        xp, st = _layer(xp, b, 0, mod_ctx, h_zero, lw, plan_p)
        states.append(st)
        xs, _ = _layer(xs, bd, grid_rows if li % 2 == 1 else 0, mods[li, :bd], state_h[:, li], lw, plan_s)

    y_prompt = jnp.transpose(xp.reshape(s, b, d), (1, 0, 2))
    y_sample = jnp.transpose(xs.reshape(l, bd, d), (1, 0, 2))
    return y_prompt, y_sample, jnp.stack(states, axis=1)


def kernel(x_prompt, x_sample, state_h, c, c_ctx, w_ada, b_ada, norm_g, w_in, conv_a_w, rnn_conv_w,
           rnn_conv_b, rnn_w_a, rnn_b_a, rnn_w_x, rnn_b_x, rnn_lam, w_out, ffn_up, ffn_conv_w, ffn_down):
    return _forward(x_prompt, x_sample, state_h, c, c_ctx, w_ada, b_ada, norm_g, w_in, conv_a_w,
                    rnn_conv_w, rnn_conv_b, rnn_w_a, rnn_b_a, rnn_w_x, rnn_b_x, rnn_lam, w_out,
                    ffn_up, ffn_conv_w, ffn_down)
```

```python
import functools

import jax
import jax.numpy as jnp
from jax import lax
from jax.experimental import pallas as pl
from jax.experimental.pallas import tpu as pltpu

GRID_W = 64
N_RNN_HEADS = 16
RGLRU_C = 8.0
N_MOD = 6
EPS = 1e-6
RNN_CONV_W = 4
CONV_A_W = 3
FFN_CONV_W = 3

SUBLANES = 8
LANES = 128
MXU_WIDTH = 256
VMEM_BYTES = 64 * 1024 * 1024
HALO_ROWS = 16
MOD_ROWS = 32

F32 = jnp.float32
BF16 = jnp.bfloat16


def _rms(x, g):
    var = jnp.mean(x * x, axis=-1, keepdims=True)
    return x * lax.rsqrt(var + EPS) * g


def _sigmoid(x):
    return 0.5 * jnp.tanh(0.5 * x) + 0.5


def _mod_body(cc_ref, w_ref, b_ref, o_ref):
    s = cc_ref[...]
    s = s * jax.nn.sigmoid(s)
    o_ref[...] = jnp.dot(s.astype(BF16), w_ref[...].astype(BF16),
                         preferred_element_type=F32) + b_ref[...]


def _modulation(cc, w_ada, b_ada, tn):
    depth, d, n = w_ada.shape
    rows = cc.shape[0]
    return pl.pallas_call(
        _mod_body,
        grid=(depth, n // tn),
        in_specs=[
            pl.BlockSpec((rows, d), lambda l, j: (0, 0)),
            pl.BlockSpec((None, d, tn), lambda l, j: (l, 0, j)),
            pl.BlockSpec((None, 1, tn), lambda l, j: (l, 0, j)),
        ],
        out_specs=pl.BlockSpec((None, rows, tn), lambda l, j: (l, 0, j)),
        out_shape=jax.ShapeDtypeStruct((depth, rows, n), F32),
        compiler_params=pltpu.CompilerParams(
            dimension_semantics=("arbitrary", "arbitrary"),
            vmem_limit_bytes=40 * 1024 * 1024),
        name="adaln_modulation",
    )(cc, w_ada, b_ada.reshape(depth, 1, n))


def _x_spec(view, tile_rows, order=lambda i: i):
    if view["kind"] == "flat":
        return pl.BlockSpec((tile_rows, view["d"]), lambda i: (order(i), 0))
    r = view["r"]
    steps = tile_rows // r
    if view["kind"] == "batch":
        return pl.BlockSpec((r, steps, view["d"]), lambda i: (0, order(i), 0))
    tpc = view["grid_rows"] // steps
    return pl.BlockSpec((steps, None, r, view["d"]), lambda i: (order(i) % tpc, order(i) // tpc, 0, 0))


def _as_view(x, view):
    if view["kind"] in ("flat", "batch"):
        return x
    return x.reshape(view["grid_rows"], GRID_W, view["r"], view["d"])


def _fill_rnn_params(par_ref, ba_ref, bx_ref, lam_ref):
    dr = par_ref.shape[1]
    par_ref[0:8, :] = jnp.broadcast_to(ba_ref[...], (SUBLANES, dr))
    par_ref[8:16, :] = jnp.broadcast_to(bx_ref[...], (SUBLANES, dr))
    par_ref[16:24, :] = jnp.broadcast_to(-RGLRU_C * jax.nn.softplus(-lam_ref[...]), (SUBLANES, dr))


def _load_state(hc_ref, nb):
    ng = hc_ref.shape[1] // LANES
    return tuple(hc_ref[sb * SUBLANES:(sb + 1) * SUBLANES, g * LANES:(g + 1) * LANES]
                 for sb in range(nb) for g in range(ng))


def _store_state(hc_ref, h, nb):
    ng = hc_ref.shape[1] // LANES
    for sb in range(nb):
        for g in range(ng):
            hc_ref[sb * SUBLANES:(sb + 1) * SUBLANES, g * LANES:(g + 1) * LANES] = h[sb * ng + g]


def _projscan_body(x_ref, sc_ref, sh_ref, g_ref, w_ref, cw_ref, cb_ref, wg_ref, ba_ref, bx_ref, lam_ref, h0_ref,
                   o_ref, hb_ref, hl_ref,
                   u_ref, xcar_ref, par_ref, hc_ref, *, r, mp, tn, xr_lo, x_batch_major):
    tm, d = u_ref.shape
    dr = hb_ref.shape[1]
    din = o_ref.shape[1]
    nb = r // SUBLANES
    ng = dr // LANES
    nblk, gw, _ = wg_ref.shape
    halo = (RNN_CONV_W - 1) * r
    npieces = tm // mp

    @pl.when(pl.program_id(0) == 0)
    def _init():
        xcar_ref[...] = jnp.zeros((halo, dr), F32)
        hc_ref[...] = h0_ref[...]
        _fill_rnn_params(par_ref, ba_ref, bx_ref, lam_ref)

    sc = jnp.concatenate([sc_ref[...]] * (mp // MOD_ROWS), axis=0)
    sh = jnp.concatenate([sh_ref[...]] * (mp // MOD_ROWS), axis=0)
    xr = []
    for p in range(npieces):
        lo = p * mp
        u = (_rms(_rows(x_ref, lo, mp, x_batch_major), g_ref[...]) * (1.0 + sc) + sh).astype(BF16)
        u_ref[lo:lo + mp, :] = u
        xr_p = jnp.dot(u, w_ref[:, xr_lo:xr_lo + dr], preferred_element_type=F32)
        o_ref[lo:lo + mp, xr_lo:xr_lo + dr] = xr_p
        xr.append(xr_p)

    def project(lo_c, hi_c):
        for lo in range(lo_c, hi_c, tn):
            hi = min(lo + tn, hi_c)
            o_ref[:, lo:hi] = jnp.dot(u_ref[...], w_ref[:, lo:hi], preferred_element_type=F32)

    xcs, pres = [], []
    for p in range(npieces):
        later = xcar_ref[...] if p == npieces - 1 else xr[p + 1][0:halo, :]
        xe = jnp.concatenate([xr[p], later], axis=0)
        xc = cb_ref[...] + cw_ref[RNN_CONV_W - 1:RNN_CONV_W, :] * xe[0:mp, :]
        for k in range(RNN_CONV_W - 1):
            off = (RNN_CONV_W - 1 - k) * r
            xc = xc + cw_ref[k:k + 1, :] * xe[off:off + mp, :]
        xcs.append(xc)
        pres.append([jnp.dot(xc[:, k * gw:(k + 1) * gw].astype(BF16), wg_ref[k], preferred_element_type=F32)
                     for k in range(nblk)])

    project(0, xr_lo)
    project(xr_lo + dr, din)

    h = list(_load_state(hc_ref, nb))
    for p in reversed(range(npieces)):
        lo = p * mp
        xc, pre = xcs[p], pres[p]
        for t in reversed(range(mp // r)):
            for sb in range(nb):
                lr = t * r + sb * SUBLANES
                rs = slice(lo + lr, lo + lr + SUBLANES)
                for g in range(ng):
                    cs = slice(g * LANES, (g + 1) * LANES)
                    k, off = divmod(g * LANES, gw)
                    hh = _rglru_unit(pre[k][lr:lr + SUBLANES, off:off + LANES],
                                     pre[k][lr:lr + SUBLANES, gw + off:gw + off + LANES],
                                     xc[lr:lr + SUBLANES, cs],
                                     par_ref[0:8, cs], par_ref[8:16, cs], par_ref[16:24, cs],
                                     h[sb * ng + g])
                    h[sb * ng + g] = hh
                    hb_ref[rs, cs] = hh

    xcar_ref[...] = xr[0][0:halo, :]
    _store_state(hc_ref, h, nb)
    hl_ref[...] = hc_ref[...]


def _projscan(x, view, sc, sh, g, w, rp, h0, r, tm, mp, tn, xr_lo):
    n, d = x.size // view["d"], view["d"]
    din = w.shape[1]
    dr = h0.shape[1]
    nt = n // tm
    const2 = lambda i: (0, 0)
    const3 = lambda i: (0, 0, 0)
    halo = (RNN_CONV_W - 1) * r
    return pl.pallas_call(
        functools.partial(_projscan_body, r=r, mp=mp, tn=tn, xr_lo=xr_lo,
                          x_batch_major=view["kind"] == "batch"),
        grid=(nt,),
        in_specs=[
            _x_spec(view, tm, order=lambda i: nt - 1 - i),
            pl.BlockSpec(sc.shape, const2),
            pl.BlockSpec(sh.shape, const2),
            pl.BlockSpec(g.shape, const2),
            pl.BlockSpec(w.shape, const2, pipeline_mode=pl.Buffered(1)),
            pl.BlockSpec(rp["cw"].shape, const2),
            pl.BlockSpec(rp["cb"].shape, const2),
            pl.BlockSpec(rp["wg"].shape, const3),
            pl.BlockSpec(rp["ba"].shape, const2),
            pl.BlockSpec(rp["bx"].shape, const2),
            pl.BlockSpec(rp["lam"].shape, const2),
            pl.BlockSpec(h0.shape, const2),
        ],
        out_specs=[
            pl.BlockSpec((tm, din), lambda i: (nt - 1 - i, 0)),
            pl.BlockSpec((tm, dr), lambda i: (nt - 1 - i, 0)),
            pl.BlockSpec(h0.shape, const2),
        ],
        out_shape=[
            jax.ShapeDtypeStruct((n, din), F32),
            jax.ShapeDtypeStruct((n, dr), F32),
            jax.ShapeDtypeStruct(h0.shape, F32),
        ],
        scratch_shapes=[
            pltpu.VMEM((tm, d), BF16),
            pltpu.VMEM((halo, dr), F32),
            pltpu.VMEM((3 * SUBLANES, dr), F32),
            pltpu.VMEM(h0.shape, F32),
        ],
        compiler_params=pltpu.CompilerParams(
            dimension_semantics=("arbitrary",),
            vmem_limit_bytes=56 * 1024 * 1024),
        name="proj_bwd_scan",
    )(_as_view(x, view), sc, sh, g, w, rp["cw"], rp["cb"], rp["wg"], rp["ba"], rp["bx"], rp["lam"], h0)


def _rows(ref, lo, n, batch_major=False):
    if len(ref.shape) == 2:
        return ref[lo:lo + n, :]
    if batch_major:
        r = ref.shape[0]
        return jnp.concatenate([ref[:, t, :] for t in range(lo // r, (lo + n) // r)], axis=0)
    r = ref.shape[1]
    return ref[lo // r:(lo + n) // r, :, :].reshape(n, ref.shape[2])


def _set_rows(ref, lo, n, val, batch_major=False):
    if len(ref.shape) == 2:
        ref[lo:lo + n, :] = val
    elif batch_major:
        r = ref.shape[0]
        for k, t in enumerate(range(lo // r, (lo + n) // r)):
            ref[:, t, :] = val[k * r:(k + 1) * r, :]
    else:
        r = ref.shape[1]
        ref[lo // r:(lo + n) // r, :, :] = val.reshape(n // r, r, ref.shape[2])


def _rglru_unit(pre_r, pre_i, xc, ba, bx, c8, h):
    gate_r = _sigmoid(pre_r + ba)
    gate_i = _sigmoid(pre_i + bx)
    a = jnp.exp(c8 * gate_r)
    s = 1.0 - a * a
    v = jnp.where(s > 0.0, s * lax.rsqrt(s), 0.0) * (gate_i * xc)
    return a * h + v


def _mix_body(bg_ref, cg_ref, xa_ref, xr_ref, gr_ref, cgh_ref, xah_ref, hb_ref, x_ref,
              cw_ref, cb_ref, wg_ref, ba_ref, bx_ref, lam_ref, h0_ref,
              caw_ref, wout_ref, gn_ref, gate_ref,
              o_ref, hl_ref,
              xcar_ref, zcar_ref, par_ref, hc_ref, y_ref, *, r, mp, x_batch_major):
    m, dr = xr_ref.shape
    dc = bg_ref.shape[1]
    nb = r // SUBLANES
    ng = dr // LANES
    nblk, gw, _ = wg_ref.shape
    halo = (RNN_CONV_W - 1) * r
    npieces = m // mp
    i = pl.program_id(0)
    last = pl.num_programs(0) - 1

    @pl.when(i == 0)
    def _init():
        xcar_ref[...] = jnp.zeros((halo, dr), F32)
        zcar_ref[...] = jnp.zeros((r, dc), F32)
        hc_ref[...] = h0_ref[...]
        _fill_rnn_params(par_ref, ba_ref, bx_ref, lam_ref)

    f32 = lambda ref, lo, n: ref[lo:lo + n, :].astype(F32)
    gate_rows = jnp.concatenate([gate_ref[...]] * (mp // MOD_ROWS), axis=0)
    h = list(_load_state(hc_ref, nb))
    for p in range(npieces):
        lo = p * mp
        if p == 0:
            xe = jnp.concatenate([xcar_ref[...], f32(xr_ref, 0, mp)], axis=0)
        else:
            xe = f32(xr_ref, lo - halo, mp + halo)
        xc = cb_ref[...] + cw_ref[0:1, :] * xe[0:mp, :]
        for k in range(1, RNN_CONV_W):
            xc = xc + cw_ref[k:k + 1, :] * xe[k * r:k * r + mp, :]
        pre = [jnp.dot(xc[:, k * gw:(k + 1) * gw].astype(BF16), wg_ref[k], preferred_element_type=F32)
               for k in range(nblk)]
        for t in range(mp // r):
            for sb in range(nb):
                lr = t * r + sb * SUBLANES
                rs = slice(lo + lr, lo + lr + SUBLANES)
                for g in range(ng):
                    cs = slice(g * LANES, (g + 1) * LANES)
                    k, off = divmod(g * LANES, gw)
                    hh = _rglru_unit(pre[k][lr:lr + SUBLANES, off:off + LANES],
                                     pre[k][lr:lr + SUBLANES, gw + off:gw + off + LANES],
                                     xc[lr:lr + SUBLANES, cs],
                                     par_ref[0:8, cs], par_ref[8:16, cs], par_ref[16:24, cs],
                                     h[sb * ng + g])
                    h[sb * ng + g] = hh
                    y_ref[rs, dc + g * LANES:dc + (g + 1) * LANES] = (
                        (hh + hb_ref[rs, cs]) * jax.nn.gelu(gr_ref[rs, cs].astype(F32)))

        z_prev = zcar_ref[...] if p == 0 else f32(cg_ref, lo - r, r) * f32(xa_ref, lo - r, r)
        if p == npieces - 1:
            z_next = jnp.where(i < last, f32(cgh_ref, 0, r) * f32(xah_ref, 0, r), 0.0)
        else:
            z_next = f32(cg_ref, lo + mp, r) * f32(xa_ref, lo + mp, r)
        ze = jnp.concatenate([z_prev, f32(cg_ref, lo, mp) * f32(xa_ref, lo, mp), z_next], axis=0)
        conv = (caw_ref[0:1, :] * ze[0:mp, :] + caw_ref[1:2, :] * ze[r:r + mp, :]
                + caw_ref[2:3, :] * ze[2 * r:2 * r + mp, :])
        y_ref[lo:lo + mp, 0:dc] = f32(bg_ref, lo, mp) * conv

        yo = jnp.dot(y_ref[lo:lo + mp, :].astype(BF16), wout_ref[...], preferred_element_type=F32)
        _set_rows(o_ref, lo, mp, _rows(x_ref, lo, mp, x_batch_major) + gate_rows * _rms(yo, gn_ref[...]))

    xcar_ref[...] = f32(xr_ref, m - halo, halo)
    zcar_ref[...] = f32(cg_ref, m - r, r) * f32(xa_ref, m - r, r)
    _store_state(hc_ref, h, nb)
    hl_ref[...] = hc_ref[...]


def _mix(proj, hb, x, view, rp, h0, caw, wout, gn, gate, r, m, mp, cols):
    n, d = proj.shape[0], view["d"]
    dr = h0.shape[1]
    dc = caw.shape[1]
    dmix = wout.shape[0]
    nt = n // m
    hr = r
    nh = n // hr
    const2 = lambda i: (0, 0)
    const3 = lambda i: (0, 0, 0)
    halo = (RNN_CONV_W - 1) * r
    col = lambda k: pl.BlockSpec((m, dc), lambda i: (i, k))
    nxt = lambda k: pl.BlockSpec((hr, dc), lambda i: (jnp.minimum((i + 1) * (m // hr), nh - 1), k))
    xspec = _x_spec(view, m)
    out_view = dict(kind="flat", d=d) if view["kind"] == "batch" else view
    out_struct = jax.ShapeDtypeStruct((n, d) if view["kind"] == "batch" else _as_view(x, view).shape, F32)
    return pl.pallas_call(
        functools.partial(_mix_body, r=r, mp=mp, x_batch_major=view["kind"] == "batch"),
        grid=(nt,),
        in_specs=[
            col(cols["bg"]), col(cols["cg"]), col(cols["xa"]), col(cols["xr"]), col(cols["gr"]),
            nxt(cols["cg"]), nxt(cols["xa"]),
            pl.BlockSpec((m, dr), lambda i: (i, 0)),
            xspec,
            pl.BlockSpec(rp["cw"].shape, const2),
            pl.BlockSpec(rp["cb"].shape, const2),
            pl.BlockSpec(rp["wg"].shape, const3),
            pl.BlockSpec(rp["ba"].shape, const2),
            pl.BlockSpec(rp["bx"].shape, const2),
            pl.BlockSpec(rp["lam"].shape, const2),
            pl.BlockSpec(h0.shape, const2),
            pl.BlockSpec(caw.shape, const2),
            pl.BlockSpec(wout.shape, const2, pipeline_mode=pl.Buffered(1)),
            pl.BlockSpec(gn.shape, const2),
            pl.BlockSpec(gate.shape, const2),
        ],
        out_specs=[
            _x_spec(out_view, m),
            pl.BlockSpec(h0.shape, const2),
        ],
        out_shape=[
            out_struct,
            jax.ShapeDtypeStruct(h0.shape, F32),
        ],
        scratch_shapes=[
            pltpu.VMEM((halo, dr), F32),
            pltpu.VMEM((r, dc), F32),
            pltpu.VMEM((3 * SUBLANES, dr), F32),
            pltpu.VMEM(h0.shape, F32),
            pltpu.VMEM((m, dmix), F32),
        ],
        compiler_params=pltpu.CompilerParams(
            dimension_semantics=("arbitrary",),
            vmem_limit_bytes=56 * 1024 * 1024),
        name="mix_fwd",
    )(proj, proj, proj, proj, proj, proj, proj, hb, _as_view(x, view),
      rp["cw"], rp["cb"], rp["wg"], rp["ba"], rp["bx"], rp["lam"], h0, caw, wout, gn, gate)


def _ffn_body(x_ref, xp_ref, xn_ref, sc_ref, sh_ref, gate_ref, gin_ref, gout_ref,
              wg_ref, wv_ref, cg_ref, cv_ref, wd_ref, o_ref, u_ref, acc_ref, *, r, ts, piece, out_batch_major):
    tm, d = x_ref.shape
    tf = wd_ref.shape[0]
    i = pl.program_id(0)
    j = pl.program_id(1)
    ni = pl.num_programs(0)
    nj = pl.num_programs(1)
    hal = HALO_ROWS
    rows_all = tm + 2 * hal

    def conv(h, c_ref, cols):
        cw = c_ref[:, cols]
        return (cw[0:1, :] * h[hal - r:hal - r + tm, :] + cw[1:2, :] * h[hal:hal + tm, :]
                + cw[2:3, :] * h[hal + r:hal + r + tm, :])

    def sub_chunk(k, hg, hv):
        cols = slice(ts * k, ts * (k + 1))
        a = (jax.nn.gelu(conv(hg, cg_ref, cols)) * conv(hv, cv_ref, cols)).astype(BF16)
        return jnp.dot(a, wd_ref[cols, :], preferred_element_type=F32)

    def up(u, k):
        cols = slice(ts * k, ts * (k + 1))
        return (jnp.dot(u, wg_ref[:, cols], preferred_element_type=F32),
                jnp.dot(u, wv_ref[:, cols], preferred_element_type=F32))

    def chunk(first):
        u = u_ref[...]
        acc = first
        for k in range(0 if first is None else 1, tf // ts):
            p = sub_chunk(k, *up(u, k))
            acc = p if acc is None else acc + p
        return acc

    @pl.when(j == 0)
    def _first_chunk():
        src = lambda lo, n: (xp_ref[...] if lo < 0 else xn_ref[...] if lo >= tm else x_ref[lo:lo + n, :])
        hg, hv = [], []
        for lo in range(0, rows_all, piece):
            n = min(piece, rows_all - lo)
            parts = []
            for q in range(lo, lo + n, hal):
                xq = src(q - hal, hal)
                uq = (_rms(xq, gin_ref[...]) * (1.0 + sc_ref[0:hal, :]) + sh_ref[0:hal, :])
                if q == 0:
                    uq = jnp.where(i > 0, uq, 0.0)
                elif q == hal + tm:
                    uq = jnp.where(i < ni - 1, uq, 0.0)
                parts.append(uq.astype(BF16))
            u_p = jnp.concatenate(parts, axis=0)
            u_ref[lo:lo + n, :] = u_p
            g_p, v_p = up(u_p, 0)
            hg.append(g_p)
            hv.append(v_p)
        first = sub_chunk(0, jnp.concatenate(hg, axis=0), jnp.concatenate(hv, axis=0))
        acc_ref[...] = chunk(first)

    @pl.when(jnp.logical_and(j > 0, j < nj - 1))
    def _middle_chunk():
        acc_ref[...] += chunk(None)

    @pl.when(jnp.logical_and(j > 0, j == nj - 1))
    def _last_chunk():
        total = acc_ref[...] + chunk(None)
        gate = jnp.concatenate([gate_ref[...]] * (piece // MOD_ROWS), axis=0)
        for lo in range(0, tm, piece):
            rows = slice(lo, lo + piece)
            _set_rows(o_ref, lo, piece, x_ref[rows, :] + gate * _rms(total[rows, :], gout_ref[...]),
                      out_batch_major)


def _ffn(x, sc, sh, gate, gin, gout, wup, cw, wd, r, tm, tf, ts, piece, out_batch_major):
    n, d = x.shape
    dff = wd.shape[0]
    assert n % tm == 0 and dff % tf == 0 and tf % ts == 0 and tm % HALO_ROWS == 0
    assert dff // tf >= 2, "the first and last hidden chunk of a tile must be different grid steps"
    ni = n // tm
    nj = dff // tf
    nh = n // HALO_ROWS
    const = lambda i, j: (0, 0)
    return pl.pallas_call(
        functools.partial(_ffn_body, r=r, ts=ts, piece=piece, out_batch_major=out_batch_major),
        grid=(ni, nj),
        in_specs=[
            pl.BlockSpec((tm, d), lambda i, j: (i, 0)),
            pl.BlockSpec((HALO_ROWS, d), lambda i, j: (jnp.maximum(i * (tm // HALO_ROWS) - 1, 0), 0)),
            pl.BlockSpec((HALO_ROWS, d), lambda i, j: (jnp.minimum((i + 1) * (tm // HALO_ROWS), nh - 1), 0)),
            pl.BlockSpec(sc.shape, const),
            pl.BlockSpec(sh.shape, const),
            pl.BlockSpec(gate.shape, const),
            pl.BlockSpec(gin.shape, const),
            pl.BlockSpec(gout.shape, const),
            pl.BlockSpec((d, tf), lambda i, j: (0, j)),
            pl.BlockSpec((d, tf), lambda i, j: (0, nj + j)),
            pl.BlockSpec((FFN_CONV_W, tf), lambda i, j: (0, j)),
            pl.BlockSpec((FFN_CONV_W, tf), lambda i, j: (0, nj + j)),
            pl.BlockSpec((tf, d), lambda i, j: (j, 0)),
        ],
        out_specs=(pl.BlockSpec((r, tm // r, d), lambda i, j: (0, i, 0)) if out_batch_major
                   else pl.BlockSpec((tm, d), lambda i, j: (i, 0))),
        out_shape=jax.ShapeDtypeStruct((r, n // r, d) if out_batch_major else (n, d), F32),
        scratch_shapes=[
            pltpu.VMEM((tm + 2 * HALO_ROWS, d), BF16),
            pltpu.VMEM((tm, d), F32),
        ],
        compiler_params=pltpu.CompilerParams(
            dimension_semantics=("arbitrary", "arbitrary"),
            vmem_limit_bytes=56 * 1024 * 1024),
        name="conv_ffn",
    )(x, x, x, sc, sh, gate, gin, gout, wup, wup, cw, cw, wd)


def _block_diag(w, gw):
    heads, hd, _ = w.shape
    hpb = gw // hd
    w = w.reshape(heads // hpb, hpb, hd, hd)
    eye = jnp.eye(hpb, dtype=w.dtype)
    return jnp.einsum("nhij,hg->nhigj", w, eye).reshape(heads // hpb, gw, gw)


def _plan(d, dr, dff, n_rows, r, grid_rows):
    seq_rows = grid_rows * r if grid_rows else n_rows
    return dict(
        proj_rows=min(256, seq_rows),
        mix_rows=min(256, seq_rows),
        ffn_rows=min(512, n_rows),
        ffn_cols=min(1024, dff),
        norm_piece=min(128, n_rows),
        mix_piece=min(128, seq_rows),
        ffn_sub=min(2 * MXU_WIDTH, dff),
        proj_cols=3 * (d - dr),
    )


def _tile_rows(mod, r, rows):
    return jnp.tile(mod, (rows // r, 1))


def _layer(x, r, grid_rows, mods, h0, lw, plan, first, last):
    d = x.shape[-1]
    n = x.size // d
    sh1, sc1, g1, sh2, sc2, g2 = [_tile_rows(m, r, MOD_ROWS) for m in jnp.split(mods, N_MOD, axis=-1)]
    if first:
        assert not grid_rows
        view = dict(kind="batch", r=r, d=d)
    elif grid_rows:
        view = dict(kind="grid", r=r, d=d, grid_rows=grid_rows)
    else:
        view = dict(kind="flat", d=d)
    ng = lw["norm_g"]
    dr = h0.shape[-1]
    proj, hb, last_b = _projscan(x, view, sc1, sh1, ng[0:1], lw["w_in"], lw["rnn"][1], h0[:, 1], r,
                                 plan["proj_rows"], plan["mix_piece"], plan["proj_cols"], lw["cols"]["xr"] * dr)
    x, last_f = _mix(proj, hb, x, view, lw["rnn"][0], h0[:, 0], lw["conv_a_w"], lw["w_out"],
                     ng[1:2], g1, r, plan["mix_rows"], plan["mix_piece"], lw["cols"])
    x = x.reshape(n, d)
    x = _ffn(x, sc2, sh2, g2, ng[2:3], ng[3:4], lw["ffn_up"], lw["ffn_conv_w"], lw["ffn_down"],
             r, plan["ffn_rows"], plan["ffn_cols"], plan["ffn_sub"], plan["norm_piece"], last)
    return x, jnp.stack([last_f, last_b], axis=1)


def _forward(x_prompt, x_sample, state_h, c, c_ctx, w_ada, b_ada, norm_g, w_in, conv_a_w,
             rnn_conv_w, rnn_conv_b, rnn_w_a, rnn_b_a, rnn_w_x, rnn_b_x, rnn_lam, w_out,
             ffn_up, ffn_conv_w, ffn_down, plan_overrides=None):
    b, s, d = x_prompt.shape
    bd, l, _ = x_sample.shape
    depth = w_ada.shape[0]
    dr = rnn_lam.shape[-1]
    dc = conv_a_w.shape[-1]
    dff = ffn_down.shape[1]
    grid_rows = l // GRID_W
    gw = min(MXU_WIDTH, dr)

    plan_p = _plan(d, dr, dff, s * b, b, 0)
    plan_s = _plan(d, dr, dff, l * bd, bd, grid_rows)
    for p in (plan_p, plan_s):
        p.update(plan_overrides or {})

    xp, xs = x_prompt, x_sample

    cc = jnp.concatenate([c, jnp.broadcast_to(c_ctx[None, :], (SUBLANES, d))], axis=0)
    mods = _modulation(cc, w_ada, b_ada, min(1024, N_MOD * d))

    h_zero = jnp.zeros((b, 2, dr), F32)
    states = []
    for li in range(depth):
        rnn = []
        for di in range(2):
            wg = jnp.concatenate([_block_diag(rnn_w_a[li, di], gw), _block_diag(rnn_w_x[li, di], gw)],
                                 axis=-1).astype(BF16)
            rnn.append(dict(cw=rnn_conv_w[li, di], cb=rnn_conv_b[li, di][None, :], wg=wg,
                            ba=rnn_b_a[li, di][None, :], bx=rnn_b_x[li, di][None, :],
                            lam=rnn_lam[li, di][None, :], xr_col=3 * dc // dr))
        lw = dict(
            norm_g=norm_g[li],
            w_in=w_in[li].astype(BF16),
            conv_a_w=conv_a_w[li],
            rnn=rnn,
            w_out=w_out[li].astype(BF16),
            ffn_up=ffn_up[li].astype(BF16),
            ffn_conv_w=ffn_conv_w[li],
            ffn_down=ffn_down[li].astype(BF16),
            cols=dict(bg=0, cg=1, xa=2, xr=3 * dc // dr, gr=3 * dc // dr + 1),
        )
        mod_ctx = jnp.broadcast_to(mods[li, bd:bd + 1], (b, N_MOD * d))
        first, last = li == 0, li == depth - 1
        xp, st = _layer(xp, b, 0, mod_ctx, h_zero, lw, plan_p, first, last)
        states.append(st)
        xs, _ = _layer(xs, bd, grid_rows if li % 2 == 1 else 0, mods[li, :bd], state_h[:, li], lw, plan_s,
                       first, last)

    return xp, xs, jnp.stack(states, axis=1)


def kernel(x_prompt, x_sample, state_h, c, c_ctx, w_ada, b_ada, norm_g, w_in, conv_a_w, rnn_conv_w,
           rnn_conv_b, rnn_w_a, rnn_b_a, rnn_w_x, rnn_b_x, rnn_lam, w_out, ffn_up, ffn_conv_w, ffn_down):
    return _forward(x_prompt, x_sample, state_h, c, c_ctx, w_ada, b_ada, norm_g, w_in, conv_a_w,
                    rnn_conv_w, rnn_conv_b, rnn_w_a, rnn_b_a, rnn_w_x, rnn_b_x, rnn_lam, w_out,
                    ffn_up, ffn_conv_w, ffn_down)
```

```python
import functools

import jax
import jax.numpy as jnp
from jax import lax
from jax.experimental import pallas as pl
from jax.experimental.pallas import tpu as pltpu

GRID_W = 64
N_RNN_HEADS = 16
RGLRU_C = 8.0
N_MOD = 6
EPS = 1e-6
RNN_CONV_W = 4
CONV_A_W = 3
FFN_CONV_W = 3

SUBLANES = 8
LANES = 128
MXU_WIDTH = 256
VMEM_BYTES = 64 * 1024 * 1024
HALO_ROWS = 16
MOD_ROWS = 32

F32 = jnp.float32
BF16 = jnp.bfloat16


def _rms(x, g):
    var = jnp.mean(x * x, axis=-1, keepdims=True)
    return x * lax.rsqrt(var + EPS) * g


def _sigmoid(x):
    return 0.5 * jnp.tanh(0.5 * x) + 0.5


def _mod_body(cc_ref, w_ref, b_ref, o_ref):
    s = cc_ref[...]
    s = s * jax.nn.sigmoid(s)
    o_ref[...] = jnp.dot(s.astype(BF16), w_ref[...].astype(BF16),
                         preferred_element_type=F32) + b_ref[...]


def _modulation(cc, w_ada, b_ada, tn):
    depth, d, n = w_ada.shape
    rows = cc.shape[0]
    return pl.pallas_call(
        _mod_body,
        grid=(depth, n // tn),
        in_specs=[
            pl.BlockSpec((rows, d), lambda l, j: (0, 0)),
            pl.BlockSpec((None, d, tn), lambda l, j: (l, 0, j)),
            pl.BlockSpec((None, 1, tn), lambda l, j: (l, 0, j)),
        ],
        out_specs=pl.BlockSpec((None, rows, tn), lambda l, j: (l, 0, j)),
        out_shape=jax.ShapeDtypeStruct((depth, rows, n), F32),
        compiler_params=pltpu.CompilerParams(
            dimension_semantics=("arbitrary", "arbitrary"),
            vmem_limit_bytes=40 * 1024 * 1024),
        name="adaln_modulation",
    )(cc, w_ada, b_ada.reshape(depth, 1, n))


def _x_spec(view, tile_rows, order=lambda i: i):
    if view["kind"] == "flat":
        return pl.BlockSpec((tile_rows, view["d"]), lambda i: (order(i), 0))
    r = view["r"]
    steps = tile_rows // r
    if view["kind"] == "batch":
        return pl.BlockSpec((r, steps, view["d"]), lambda i: (0, order(i), 0))
    tpc = view["grid_rows"] // steps
    return pl.BlockSpec((steps, None, r, view["d"]), lambda i: (order(i) % tpc, order(i) // tpc, 0, 0))


def _as_view(x, view):
    if view["kind"] in ("flat", "batch"):
        return x
    return x.reshape(view["grid_rows"], GRID_W, view["r"], view["d"])


def _fill_rnn_params(par_ref, ba_ref, bx_ref, lam_ref):
    dr = par_ref.shape[1]
    par_ref[0:8, :] = jnp.broadcast_to(ba_ref[...], (SUBLANES, dr))
    par_ref[8:16, :] = jnp.broadcast_to(bx_ref[...], (SUBLANES, dr))
    par_ref[16:24, :] = jnp.broadcast_to(-RGLRU_C * jax.nn.softplus(-lam_ref[...]), (SUBLANES, dr))


def _load_state(hc_ref, nb):
    ng = hc_ref.shape[1] // LANES
    return tuple(hc_ref[sb * SUBLANES:(sb + 1) * SUBLANES, g * LANES:(g + 1) * LANES]
                 for sb in range(nb) for g in range(ng))


def _store_state(hc_ref, h, nb):
    ng = hc_ref.shape[1] // LANES
    for sb in range(nb):
        for g in range(ng):
            hc_ref[sb * SUBLANES:(sb + 1) * SUBLANES, g * LANES:(g + 1) * LANES] = h[sb * ng + g]


def _projscan_body(x_ref, sc_ref, sh_ref, g_ref, w_ref, cw_ref, cb_ref, wg_ref, ba_ref, bx_ref, lam_ref, h0_ref,
                   o_ref, hb_ref, hl_ref,
                   u_ref, xcar_ref, par_ref, hc_ref, *, r, mp, tn, xr_lo, x_batch_major):
    tm, d = u_ref.shape
    dr = hb_ref.shape[1]
    din = o_ref.shape[1]
    nb = r // SUBLANES
    ng = dr // LANES
    nblk, gw, _ = wg_ref.shape
    halo = (RNN_CONV_W - 1) * r
    npieces = tm // mp

    @pl.when(pl.program_id(0) == 0)
    def _init():
        xcar_ref[...] = jnp.zeros((halo, dr), F32)
        hc_ref[...] = h0_ref[...]
        _fill_rnn_params(par_ref, ba_ref, bx_ref, lam_ref)

    sc = jnp.concatenate([sc_ref[...]] * (mp // MOD_ROWS), axis=0)
    sh = jnp.concatenate([sh_ref[...]] * (mp // MOD_ROWS), axis=0)
    xr = []
    for p in range(npieces):
        lo = p * mp
        u = (_rms(_rows(x_ref, lo, mp, x_batch_major), g_ref[...]) * (1.0 + sc) + sh).astype(BF16)
        u_ref[lo:lo + mp, :] = u
        xr_p = jnp.dot(u, w_ref[:, xr_lo:xr_lo + dr], preferred_element_type=F32)
        o_ref[lo:lo + mp, xr_lo:xr_lo + dr] = xr_p
        xr.append(xr_p)

    def project(lo_c, hi_c):
        for lo in range(lo_c, hi_c, tn):
            hi = min(lo + tn, hi_c)
            o_ref[:, lo:hi] = jnp.dot(u_ref[...], w_ref[:, lo:hi], preferred_element_type=F32)

    xcs, pres = [], []
    for p in range(npieces):
        later = xcar_ref[...] if p == npieces - 1 else xr[p + 1][0:halo, :]
        xe = jnp.concatenate([xr[p], later], axis=0)
        xc = cb_ref[...] + cw_ref[RNN_CONV_W - 1:RNN_CONV_W, :] * xe[0:mp, :]
        for k in range(RNN_CONV_W - 1):
            off = (RNN_CONV_W - 1 - k) * r
            xc = xc + cw_ref[k:k + 1, :] * xe[off:off + mp, :]
        xcs.append(xc)
        pres.append([jnp.dot(xc[:, k * gw:(k + 1) * gw].astype(BF16), wg_ref[k], preferred_element_type=F32)
                     for k in range(nblk)])

    project(0, xr_lo)
    project(xr_lo + dr, din)

    h = list(_load_state(hc_ref, nb))
    for p in reversed(range(npieces)):
        lo = p * mp
        xc, pre = xcs[p], pres[p]
        for t in reversed(range(mp // r)):
            for sb in range(nb):
                lr = t * r + sb * SUBLANES
                rs = slice(lo + lr, lo + lr + SUBLANES)
                for g in range(ng):
                    cs = slice(g * LANES, (g + 1) * LANES)
                    k, off = divmod(g * LANES, gw)
                    hh = _rglru_unit(pre[k][lr:lr + SUBLANES, off:off + LANES],
                                     pre[k][lr:lr + SUBLANES, gw + off:gw + off + LANES],
                                     xc[lr:lr + SUBLANES, cs],
                                     par_ref[0:8, cs], par_ref[8:16, cs], par_ref[16:24, cs],
                                     h[sb * ng + g])
                    h[sb * ng + g] = hh
                    hb_ref[rs, cs] = hh

    xcar_ref[...] = xr[0][0:halo, :]
    _store_state(hc_ref, h, nb)
    hl_ref[...] = hc_ref[...]


def _projscan(x, view, sc, sh, g, w, li, rp, h0, r, tm, mp, tn, xr_lo):
    n, d = x.size // view["d"], view["d"]
    din = w.shape[-1]
    dr = h0.shape[1]
    nt = n // tm
    const2 = lambda i: (0, 0)
    const3 = lambda i: (0, 0, 0)
    halo = (RNN_CONV_W - 1) * r
    return pl.pallas_call(
        functools.partial(_projscan_body, r=r, mp=mp, tn=tn, xr_lo=xr_lo,
                          x_batch_major=view["kind"] == "batch"),
        grid=(nt,),
        in_specs=[
            _x_spec(view, tm, order=lambda i: nt - 1 - i),
            pl.BlockSpec(sc.shape, const2),
            pl.BlockSpec(sh.shape, const2),
            pl.BlockSpec(g.shape, const2),
            pl.BlockSpec((None,) + w.shape[1:], lambda i: (li, 0, 0), pipeline_mode=pl.Buffered(1)),
            pl.BlockSpec(rp["cw"].shape, const2),
            pl.BlockSpec(rp["cb"].shape, const2),
            pl.BlockSpec(rp["wg"].shape, const3),
            pl.BlockSpec(rp["ba"].shape, const2),
            pl.BlockSpec(rp["bx"].shape, const2),
            pl.BlockSpec(rp["lam"].shape, const2),
            pl.BlockSpec(h0.shape, const2),
        ],
        out_specs=[
            pl.BlockSpec((tm, din), lambda i: (nt - 1 - i, 0)),
            pl.BlockSpec((tm, dr), lambda i: (nt - 1 - i, 0)),
            pl.BlockSpec(h0.shape, const2),
        ],
        out_shape=[
            jax.ShapeDtypeStruct((n, din), F32),
            jax.ShapeDtypeStruct((n, dr), F32),
            jax.ShapeDtypeStruct(h0.shape, F32),
        ],
        scratch_shapes=[
            pltpu.VMEM((tm, d), BF16),
            pltpu.VMEM((halo, dr), F32),
            pltpu.VMEM((3 * SUBLANES, dr), F32),
            pltpu.VMEM(h0.shape, F32),
        ],
        compiler_params=pltpu.CompilerParams(
            dimension_semantics=("arbitrary",),
            vmem_limit_bytes=56 * 1024 * 1024),
        name="proj_bwd_scan",
    )(_as_view(x, view), sc, sh, g, w, rp["cw"], rp["cb"], rp["wg"], rp["ba"], rp["bx"], rp["lam"], h0)


def _rows(ref, lo, n, batch_major=False):
    if len(ref.shape) == 2:
        return ref[lo:lo + n, :]
    if batch_major:
        r = ref.shape[0]
        return jnp.concatenate([ref[:, t, :] for t in range(lo // r, (lo + n) // r)], axis=0)
    r = ref.shape[1]
    return ref[lo // r:(lo + n) // r, :, :].reshape(n, ref.shape[2])


def _set_rows(ref, lo, n, val, batch_major=False):
    if len(ref.shape) == 2:
        ref[lo:lo + n, :] = val
    elif batch_major:
        r = ref.shape[0]
        for k, t in enumerate(range(lo // r, (lo + n) // r)):
            ref[:, t, :] = val[k * r:(k + 1) * r, :]
    else:
        r = ref.shape[1]
        ref[lo // r:(lo + n) // r, :, :] = val.reshape(n // r, r, ref.shape[2])


def _rglru_unit(pre_r, pre_i, xc, ba, bx, c8, h):
    gate_r = _sigmoid(pre_r + ba)
    gate_i = _sigmoid(pre_i + bx)
    a = jnp.exp(c8 * gate_r)
    s = 1.0 - a * a
    v = jnp.where(s > 0.0, s * lax.rsqrt(s), 0.0) * (gate_i * xc)
    return a * h + v


def _mix_body(bg_ref, cg_ref, xa_ref, xr_ref, gr_ref, cgh_ref, xah_ref, hb_ref, x_ref,
              cw_ref, cb_ref, wg_ref, ba_ref, bx_ref, lam_ref, h0_ref,
              caw_ref, wout_ref, gn_ref, gate_ref,
              o_ref, hl_ref,
              xcar_ref, zcar_ref, par_ref, hc_ref, y_ref, *, r, mp, x_batch_major):
    m, dr = xr_ref.shape
    dc = bg_ref.shape[1]
    nb = r // SUBLANES
    ng = dr // LANES
    nblk, gw, _ = wg_ref.shape
    halo = (RNN_CONV_W - 1) * r
    npieces = m // mp
    i = pl.program_id(0)
    last = pl.num_programs(0) - 1

    @pl.when(i == 0)
    def _init():
        xcar_ref[...] = jnp.zeros((halo, dr), F32)
        zcar_ref[...] = jnp.zeros((r, dc), F32)
        hc_ref[...] = h0_ref[...]
        _fill_rnn_params(par_ref, ba_ref, bx_ref, lam_ref)

    f32 = lambda ref, lo, n: ref[lo:lo + n, :].astype(F32)
    gate_rows = jnp.concatenate([gate_ref[...]] * (mp // MOD_ROWS), axis=0)
    h = list(_load_state(hc_ref, nb))
    for p in range(npieces):
        lo = p * mp
        if p == 0:
            xe = jnp.concatenate([xcar_ref[...], f32(xr_ref, 0, mp)], axis=0)
        else:
            xe = f32(xr_ref, lo - halo, mp + halo)
        xc = cb_ref[...] + cw_ref[0:1, :] * xe[0:mp, :]
        for k in range(1, RNN_CONV_W):
            xc = xc + cw_ref[k:k + 1, :] * xe[k * r:k * r + mp, :]
        pre = [jnp.dot(xc[:, k * gw:(k + 1) * gw].astype(BF16), wg_ref[k], preferred_element_type=F32)
               for k in range(nblk)]
        for t in range(mp // r):
            for sb in range(nb):
                lr = t * r + sb * SUBLANES
                rs = slice(lo + lr, lo + lr + SUBLANES)
                for g in range(ng):
                    cs = slice(g * LANES, (g + 1) * LANES)
                    k, off = divmod(g * LANES, gw)
                    hh = _rglru_unit(pre[k][lr:lr + SUBLANES, off:off + LANES],
                                     pre[k][lr:lr + SUBLANES, gw + off:gw + off + LANES],
                                     xc[lr:lr + SUBLANES, cs],
                                     par_ref[0:8, cs], par_ref[8:16, cs], par_ref[16:24, cs],
                                     h[sb * ng + g])
                    h[sb * ng + g] = hh
                    y_ref[rs, dc + g * LANES:dc + (g + 1) * LANES] = (
                        (hh + hb_ref[rs, cs]) * jax.nn.gelu(gr_ref[rs, cs].astype(F32)))

        z_prev = zcar_ref[...] if p == 0 else f32(cg_ref, lo - r, r) * f32(xa_ref, lo - r, r)
        if p == npieces - 1:
            z_next = jnp.where(i < last, f32(cgh_ref, 0, r) * f32(xah_ref, 0, r), 0.0)
        else:
            z_next = f32(cg_ref, lo + mp, r) * f32(xa_ref, lo + mp, r)
        ze = jnp.concatenate([z_prev, f32(cg_ref, lo, mp) * f32(xa_ref, lo, mp), z_next], axis=0)
        conv = (caw_ref[0:1, :] * ze[0:mp, :] + caw_ref[1:2, :] * ze[r:r + mp, :]
                + caw_ref[2:3, :] * ze[2 * r:2 * r + mp, :])
        y_ref[lo:lo + mp, 0:dc] = f32(bg_ref, lo, mp) * conv

        yo = jnp.dot(y_ref[lo:lo + mp, :].astype(BF16), wout_ref[...], preferred_element_type=F32)
        _set_rows(o_ref, lo, mp, _rows(x_ref, lo, mp, x_batch_major) + gate_rows * _rms(yo, gn_ref[...]))

    xcar_ref[...] = f32(xr_ref, m - halo, halo)
    zcar_ref[...] = f32(cg_ref, m - r, r) * f32(xa_ref, m - r, r)
    _store_state(hc_ref, h, nb)
    hl_ref[...] = hc_ref[...]


def _mix(proj, hb, x, view, rp, h0, caw, wout, li, gn, gate, r, m, mp, cols):
    n, d = proj.shape[0], view["d"]
    dr = h0.shape[1]
    dc = caw.shape[1]
    dmix = wout.shape[-2]
    nt = n // m
    hr = r
    nh = n // hr
    const2 = lambda i: (0, 0)
    const3 = lambda i: (0, 0, 0)
    halo = (RNN_CONV_W - 1) * r
    col = lambda k: pl.BlockSpec((m, dc), lambda i: (i, k))
    nxt = lambda k: pl.BlockSpec((hr, dc), lambda i: (jnp.minimum((i + 1) * (m // hr), nh - 1), k))
    xspec = _x_spec(view, m)
    out_view = dict(kind="flat", d=d) if view["kind"] == "batch" else view
    out_struct = jax.ShapeDtypeStruct((n, d) if view["kind"] == "batch" else _as_view(x, view).shape, F32)
    return pl.pallas_call(
        functools.partial(_mix_body, r=r, mp=mp, x_batch_major=view["kind"] == "batch"),
        grid=(nt,),
        in_specs=[
            col(cols["bg"]), col(cols["cg"]), col(cols["xa"]), col(cols["xr"]), col(cols["gr"]),
            nxt(cols["cg"]), nxt(cols["xa"]),
            pl.BlockSpec((m, dr), lambda i: (i, 0)),
            xspec,
            pl.BlockSpec(rp["cw"].shape, const2),
            pl.BlockSpec(rp["cb"].shape, const2),
            pl.BlockSpec(rp["wg"].shape, const3),
            pl.BlockSpec(rp["ba"].shape, const2),
            pl.BlockSpec(rp["bx"].shape, const2),
            pl.BlockSpec(rp["lam"].shape, const2),
            pl.BlockSpec(h0.shape, const2),
            pl.BlockSpec(caw.shape, const2),
            pl.BlockSpec((None,) + wout.shape[1:], lambda i: (li, 0, 0), pipeline_mode=pl.Buffered(1)),
            pl.BlockSpec(gn.shape, const2),
            pl.BlockSpec(gate.shape, const2),
        ],
        out_specs=[
            _x_spec(out_view, m),
            pl.BlockSpec(h0.shape, const2),
        ],
        out_shape=[
            out_struct,
            jax.ShapeDtypeStruct(h0.shape, F32),
        ],
        scratch_shapes=[
            pltpu.VMEM((halo, dr), F32),
            pltpu.VMEM((r, dc), F32),
            pltpu.VMEM((3 * SUBLANES, dr), F32),
            pltpu.VMEM(h0.shape, F32),
            pltpu.VMEM((m, dmix), F32),
        ],
        compiler_params=pltpu.CompilerParams(
            dimension_semantics=("arbitrary",),
            vmem_limit_bytes=56 * 1024 * 1024),
        name="mix_fwd",
    )(proj, proj, proj, proj, proj, proj, proj, hb, _as_view(x, view),
      rp["cw"], rp["cb"], rp["wg"], rp["ba"], rp["bx"], rp["lam"], h0, caw, wout, gn, gate)


def _ffn_body(x_ref, xp_ref, xn_ref, sc_ref, sh_ref, gate_ref, gin_ref, gout_ref,
              wg_ref, wv_ref, cg_ref, cv_ref, wd_ref, o_ref, u_ref, acc_ref, *, r, ts, piece, out_batch_major):
    tm, d = x_ref.shape
    tf = wd_ref.shape[0]
    i = pl.program_id(0)
    j = pl.program_id(1)
    ni = pl.num_programs(0)
    nj = pl.num_programs(1)
    hal = HALO_ROWS
    rows_all = tm + 2 * r

    def conv(h, c_ref, cols):
        cw = c_ref[:, cols]
        h_prev = jnp.concatenate([h[tm:tm + r, :], h[0:tm - r, :]], axis=0)
        h_next = jnp.concatenate([h[r:tm, :], h[tm + r:tm + 2 * r, :]], axis=0)
        return cw[0:1, :] * h_prev + cw[1:2, :] * h[0:tm, :] + cw[2:3, :] * h_next

    def sub_chunk(k, hg, hv):
        cols = slice(ts * k, ts * (k + 1))
        a = (jax.nn.gelu(conv(hg, cg_ref, cols)) * conv(hv, cv_ref, cols)).astype(BF16)
        return jnp.dot(a, wd_ref[cols, :], preferred_element_type=F32)

    def up(u, k):
        cols = slice(ts * k, ts * (k + 1))
        return (jnp.dot(u, wg_ref[:, cols], preferred_element_type=F32),
                jnp.dot(u, wv_ref[:, cols], preferred_element_type=F32))

    def chunk(first):
        u = u_ref[...]
        acc = first
        for k in range(0 if first is None else 1, tf // ts):
            p = sub_chunk(k, *up(u, k))
            acc = p if acc is None else acc + p
        return acc

    @pl.when(j == 0)
    def _first_chunk():
        def modnorm(xq):
            n = xq.shape[0]
            return _rms(xq, gin_ref[...]) * (1.0 + sc_ref[0:n, :]) + sh_ref[0:n, :]

        u_halo = modnorm(jnp.concatenate([xp_ref[hal - r:hal, :], xn_ref[0:r, :]], axis=0))
        u_halo = jnp.concatenate([jnp.where(i > 0, u_halo[0:r, :], 0.0),
                                  jnp.where(i < ni - 1, u_halo[r:2 * r, :], 0.0)], axis=0).astype(BF16)
        hg, hv = [], []
        for lo in range(0, tm, piece):
            n = min(piece, tm - lo)
            parts = [modnorm(x_ref[q:q + hal, :]).astype(BF16) for q in range(lo, lo + n, hal)]
            if lo + n == tm:
                parts.append(u_halo)
                n += 2 * r
            u_p = jnp.concatenate(parts, axis=0)
            u_ref[lo:lo + n, :] = u_p
            g_p, v_p = up(u_p, 0)
            hg.append(g_p)
            hv.append(v_p)
        first = sub_chunk(0, jnp.concatenate(hg, axis=0), jnp.concatenate(hv, axis=0))
        acc_ref[...] = chunk(first)

    @pl.when(jnp.logical_and(j > 0, j < nj - 1))
    def _middle_chunk():
        acc_ref[...] += chunk(None)

    @pl.when(jnp.logical_and(j > 0, j == nj - 1))
    def _last_chunk():
        total = acc_ref[...] + chunk(None)
        gate = jnp.concatenate([gate_ref[...]] * (piece // MOD_ROWS), axis=0)
        for lo in range(0, tm, piece):
            rows = slice(lo, lo + piece)
            _set_rows(o_ref, lo, piece, x_ref[rows, :] + gate * _rms(total[rows, :], gout_ref[...]),
                      out_batch_major)


def _ffn(x, sc, sh, gate, gin, gout, wup, cw, wd, li, r, tm, tf, ts, piece, out_batch_major):
    n, d = x.shape
    dff = wd.shape[-2]
    assert n % tm == 0 and dff % tf == 0 and tf % ts == 0 and tm % HALO_ROWS == 0
    assert dff // tf >= 2, "the first and last hidden chunk of a tile must be different grid steps"
    ni = n // tm
    nj = dff // tf
    nh = n // HALO_ROWS
    const = lambda i, j: (0, 0)
    return pl.pallas_call(
        functools.partial(_ffn_body, r=r, ts=ts, piece=piece, out_batch_major=out_batch_major),
        grid=(ni, nj),
        in_specs=[
            pl.BlockSpec((tm, d), lambda i, j: (i, 0)),
            pl.BlockSpec((HALO_ROWS, d), lambda i, j: (jnp.maximum(i * (tm // HALO_ROWS) - 1, 0), 0)),
            pl.BlockSpec((HALO_ROWS, d), lambda i, j: (jnp.minimum((i + 1) * (tm // HALO_ROWS), nh - 1), 0)),
            pl.BlockSpec(sc.shape, const),
            pl.BlockSpec(sh.shape, const),
            pl.BlockSpec(gate.shape, const),
            pl.BlockSpec(gin.shape, const),
            pl.BlockSpec(gout.shape, const),
            pl.BlockSpec((None, d, tf), lambda i, j: (li, 0, j)),
            pl.BlockSpec((None, d, tf), lambda i, j: (li, 0, nj + j)),
            pl.BlockSpec((FFN_CONV_W, tf), lambda i, j: (0, j)),
            pl.BlockSpec((FFN_CONV_W, tf), lambda i, j: (0, nj + j)),
            pl.BlockSpec((None, tf, d), lambda i, j: (li, j, 0)),
        ],
        out_specs=(pl.BlockSpec((r, tm // r, d), lambda i, j: (0, i, 0)) if out_batch_major
                   else pl.BlockSpec((tm, d), lambda i, j: (i, 0))),
        out_shape=jax.ShapeDtypeStruct((r, n // r, d) if out_batch_major else (n, d), F32),
        scratch_shapes=[
            pltpu.VMEM((tm + 2 * r, d), BF16),
            pltpu.VMEM((tm, d), F32),
        ],
        compiler_params=pltpu.CompilerParams(
            dimension_semantics=("arbitrary", "arbitrary"),
            vmem_limit_bytes=56 * 1024 * 1024),
        name="conv_ffn",
    )(x, x, x, sc, sh, gate, gin, gout, wup, wup, cw, cw, wd)


def _block_diag(w, gw):
    heads, hd, _ = w.shape
    hpb = gw // hd
    w = w.reshape(heads // hpb, hpb, hd, hd)
    eye = jnp.eye(hpb, dtype=w.dtype)
    return jnp.einsum("nhij,hg->nhigj", w, eye).reshape(heads // hpb, gw, gw)


def _plan(d, dr, dff, n_rows, r, grid_rows):
    seq_rows = grid_rows * r if grid_rows else n_rows
    return dict(
        proj_rows=min(256, seq_rows),
        mix_rows=min(256, seq_rows),
        ffn_rows=min(512, n_rows),
        ffn_cols=min(1024, dff),
        norm_piece=min(128, n_rows),
        mix_piece=min(128, seq_rows),
        ffn_sub=min(2 * MXU_WIDTH, dff),
        proj_cols=3 * (d - dr),
    )


def _tile_rows(mod, r, rows):
    return jnp.tile(mod, (rows // r, 1))


def _layer(x, r, grid_rows, mods, h0, lw, plan, first, last):
    d = x.shape[-1]
    n = x.size // d
    sh1, sc1, g1, sh2, sc2, g2 = [_tile_rows(m, r, MOD_ROWS) for m in jnp.split(mods, N_MOD, axis=-1)]
    if first:
        assert not grid_rows
        view = dict(kind="batch", r=r, d=d)
    elif grid_rows:
        view = dict(kind="grid", r=r, d=d, grid_rows=grid_rows)
    else:
        view = dict(kind="flat", d=d)
    ng = lw["norm_g"]
    dr = h0.shape[-1]
    proj, hb, last_b = _projscan(x, view, sc1, sh1, ng[0:1], lw["w_in"], lw["li"], lw["rnn"][1], h0[:, 1], r,
                                 plan["proj_rows"], plan["mix_piece"], plan["proj_cols"], lw["cols"]["xr"] * dr)
    x, last_f = _mix(proj, hb, x, view, lw["rnn"][0], h0[:, 0], lw["conv_a_w"], lw["w_out"], lw["li"],
                     ng[1:2], g1, r, plan["mix_rows"], plan["mix_piece"], lw["cols"])
    x = x.reshape(n, d)
    x = _ffn(x, sc2, sh2, g2, ng[2:3], ng[3:4], lw["ffn_up"], lw["ffn_conv_w"], lw["ffn_down"], lw["li"],
             r, plan["ffn_rows"], plan["ffn_cols"], plan["ffn_sub"], plan["norm_piece"], last)
    return x, jnp.stack([last_f, last_b], axis=1)


def _forward(x_prompt, x_sample, state_h, c, c_ctx, w_ada, b_ada, norm_g, w_in, conv_a_w,
             rnn_conv_w, rnn_conv_b, rnn_w_a, rnn_b_a, rnn_w_x, rnn_b_x, rnn_lam, w_out,
             ffn_up, ffn_conv_w, ffn_down, plan_overrides=None):
    b, s, d = x_prompt.shape
    bd, l, _ = x_sample.shape
    depth = w_ada.shape[0]
    dr = rnn_lam.shape[-1]
    dc = conv_a_w.shape[-1]
    dff = ffn_down.shape[1]
    grid_rows = l // GRID_W
    gw = min(MXU_WIDTH, dr)

    plan_p = _plan(d, dr, dff, s * b, b, 0)
    plan_s = _plan(d, dr, dff, l * bd, bd, grid_rows)
    for p in (plan_p, plan_s):
        p.update(plan_overrides or {})

    xp, xs = x_prompt, x_sample

    cc = jnp.concatenate([c, jnp.broadcast_to(c_ctx[None, :], (SUBLANES, d))], axis=0)
    mods = _modulation(cc, w_ada, b_ada, min(1024, N_MOD * d))

    h_zero = jnp.zeros((b, 2, dr), F32)
    w_in_b, w_out_b, ffn_up_b, ffn_down_b = (w.astype(BF16) for w in (w_in, w_out, ffn_up, ffn_down))
    states = []
    for li in range(depth):
        rnn = []
        for di in range(2):
            wg = jnp.concatenate([_block_diag(rnn_w_a[li, di], gw), _block_diag(rnn_w_x[li, di], gw)],
                                 axis=-1).astype(BF16)
            rnn.append(dict(cw=rnn_conv_w[li, di], cb=rnn_conv_b[li, di][None, :], wg=wg,
                            ba=rnn_b_a[li, di][None, :], bx=rnn_b_x[li, di][None, :],
                            lam=rnn_lam[li, di][None, :], xr_col=3 * dc // dr))
        lw = dict(
            norm_g=norm_g[li],
            li=li,
            w_in=w_in_b,
            conv_a_w=conv_a_w[li],
            rnn=rnn,
            w_out=w_out_b,
            ffn_up=ffn_up_b,
            ffn_conv_w=ffn_conv_w[li],
            ffn_down=ffn_down_b,
            cols=dict(bg=0, cg=1, xa=2, xr=3 * dc // dr, gr=3 * dc // dr + 1),
        )
        mod_ctx = jnp.broadcast_to(mods[li, bd:bd + 1], (b, N_MOD * d))
        first, last = li == 0, li == depth - 1
        xp, st = _layer(xp, b, 0, mod_ctx, h_zero, lw, plan_p, first, last)
        states.append(st)
        xs, _ = _layer(xs, bd, grid_rows if li % 2 == 1 else 0, mods[li, :bd], state_h[:, li], lw, plan_s,
                       first, last)

    return xp, xs, jnp.stack(states, axis=1)


def kernel(x_prompt, x_sample, state_h, c, c_ctx, w_ada, b_ada, norm_g, w_in, conv_a_w, rnn_conv_w,
           rnn_conv_b, rnn_w_a, rnn_b_a, rnn_w_x, rnn_b_x, rnn_lam, w_out, ffn_up, ffn_conv_w, ffn_down):
    return _forward(x_prompt, x_sample, state_h, c, c_ctx, w_ada, b_ada, norm_g, w_in, conv_a_w,
                    rnn_conv_w, rnn_conv_b, rnn_w_a, rnn_b_a, rnn_w_x, rnn_b_x, rnn_lam, w_out,
                    ffn_up, ffn_conv_w, ffn_down)
```

```python
import functools

import jax
import jax.numpy as jnp
from jax import lax
from jax.experimental import pallas as pl
from jax.experimental.pallas import tpu as pltpu

GRID_W = 64
N_RNN_HEADS = 16
RGLRU_C = 8.0
N_MOD = 6
EPS = 1e-6
RNN_CONV_W = 4
CONV_A_W = 3
FFN_CONV_W = 3

SUBLANES = 8
LANES = 128
MXU_WIDTH = 256
VMEM_BYTES = 64 * 1024 * 1024
HALO_ROWS = 16
MOD_ROWS = 32

F32 = jnp.float32
BF16 = jnp.bfloat16


def _rms(x, g):
    var = jnp.mean(x * x, axis=-1, keepdims=True)
    return x * lax.rsqrt(var + EPS) * g


def _sigmoid(x):
    return 0.5 * jnp.tanh(0.5 * x) + 0.5


def _mod_body(cc_ref, w_ref, b_ref, o_ref):
    s = cc_ref[...]
    s = s * jax.nn.sigmoid(s)
    o_ref[...] = jnp.dot(s.astype(BF16), w_ref[...].astype(BF16),
                         preferred_element_type=F32) + b_ref[...]


def _modulation(cc, w_ada, b_ada, tn):
    depth, d, n = w_ada.shape
    rows = cc.shape[0]
    return pl.pallas_call(
        _mod_body,
        grid=(depth, n // tn),
        in_specs=[
            pl.BlockSpec((rows, d), lambda l, j: (0, 0)),
            pl.BlockSpec((None, d, tn), lambda l, j: (l, 0, j)),
            pl.BlockSpec((None, 1, tn), lambda l, j: (l, 0, j)),
        ],
        out_specs=pl.BlockSpec((None, rows, tn), lambda l, j: (l, 0, j)),
        out_shape=jax.ShapeDtypeStruct((depth, rows, n), F32),
        compiler_params=pltpu.CompilerParams(
            dimension_semantics=("arbitrary", "arbitrary"),
            vmem_limit_bytes=40 * 1024 * 1024),
        name="adaln_modulation",
    )(cc, w_ada, b_ada.reshape(depth, 1, n))


def _x_spec(view, tile_rows, order=lambda i: i):
    if view["kind"] == "flat":
        return pl.BlockSpec((tile_rows, view["d"]), lambda i: (order(i), 0))
    r = view["r"]
    steps = tile_rows // r
    if view["kind"] == "batch":
        return pl.BlockSpec((r, steps, view["d"]), lambda i: (0, order(i), 0))
    tpc = view["grid_rows"] // steps
    return pl.BlockSpec((steps, None, r, view["d"]), lambda i: (order(i) % tpc, order(i) // tpc, 0, 0))


def _as_view(x, view):
    if view["kind"] in ("flat", "batch"):
        return x
    return x.reshape(view["grid_rows"], GRID_W, view["r"], view["d"])


def _fill_rnn_params(par_ref, ba_ref, bx_ref, lam_ref):
    dr = par_ref.shape[1]
    par_ref[0:8, :] = jnp.broadcast_to(ba_ref[...], (SUBLANES, dr))
    par_ref[8:16, :] = jnp.broadcast_to(bx_ref[...], (SUBLANES, dr))
    par_ref[16:24, :] = jnp.broadcast_to(-RGLRU_C * jax.nn.softplus(-lam_ref[...]), (SUBLANES, dr))


def _load_state(hc_ref, nb):
    ng = hc_ref.shape[1] // LANES
    return tuple(hc_ref[sb * SUBLANES:(sb + 1) * SUBLANES, g * LANES:(g + 1) * LANES]
                 for sb in range(nb) for g in range(ng))


def _store_state(hc_ref, h, nb):
    ng = hc_ref.shape[1] // LANES
    for sb in range(nb):
        for g in range(ng):
            hc_ref[sb * SUBLANES:(sb + 1) * SUBLANES, g * LANES:(g + 1) * LANES] = h[sb * ng + g]


def _projscan_body(x_ref, sc_ref, sh_ref, g_ref, w_ref, cw_ref, cb_ref, wg_ref, ba_ref, bx_ref, lam_ref, h0_ref,
                   o_ref, hb_ref, hl_ref,
                   u_ref, xcar_ref, par_ref, hc_ref, *, r, mp, tn, xr_lo, x_batch_major):
    tm, d = u_ref.shape
    dr = hb_ref.shape[1]
    din = o_ref.shape[1]
    nb = r // SUBLANES
    ng = dr // LANES
    nblk, gw, _ = wg_ref.shape
    halo = (RNN_CONV_W - 1) * r
    npieces = tm // mp

    @pl.when(pl.program_id(0) == 0)
    def _init():
        xcar_ref[...] = jnp.zeros((halo, dr), F32)
        hc_ref[...] = h0_ref[...]
        _fill_rnn_params(par_ref, ba_ref, bx_ref, lam_ref)

    sc = jnp.concatenate([sc_ref[...]] * (mp // MOD_ROWS), axis=0)
    sh = jnp.concatenate([sh_ref[...]] * (mp // MOD_ROWS), axis=0)
    xr = []
    for p in range(npieces):
        lo = p * mp
        u = (_rms(_rows(x_ref, lo, mp, x_batch_major), g_ref[...]) * (1.0 + sc) + sh).astype(BF16)
        u_ref[lo:lo + mp, :] = u
        xr_p = jnp.dot(u, w_ref[:, xr_lo:xr_lo + dr], preferred_element_type=F32)
        o_ref[lo:lo + mp, xr_lo:xr_lo + dr] = xr_p
        xr.append(xr_p)

    def project(lo_c, hi_c):
        for lo in range(lo_c, hi_c, tn):
            hi = min(lo + tn, hi_c)
            o_ref[:, lo:hi] = jnp.dot(u_ref[...], w_ref[:, lo:hi], preferred_element_type=F32)

    xcs, pres = [], []
    for p in range(npieces):
        later = xcar_ref[...] if p == npieces - 1 else xr[p + 1][0:halo, :]
        xe = jnp.concatenate([xr[p], later], axis=0)
        xc = cb_ref[...] + cw_ref[RNN_CONV_W - 1:RNN_CONV_W, :] * xe[0:mp, :]
        for k in range(RNN_CONV_W - 1):
            off = (RNN_CONV_W - 1 - k) * r
            xc = xc + cw_ref[k:k + 1, :] * xe[off:off + mp, :]
        xcs.append(xc)
        pres.append([jnp.dot(xc[:, k * gw:(k + 1) * gw].astype(BF16), wg_ref[k], preferred_element_type=F32)
                     for k in range(nblk)])

    project(0, xr_lo)
    project(xr_lo + dr, din)

    h = list(_load_state(hc_ref, nb))
    for p in reversed(range(npieces)):
        lo = p * mp
        xc, pre = xcs[p], pres[p]
        for t in reversed(range(mp // r)):
            for sb in range(nb):
                lr = t * r + sb * SUBLANES
                rs = slice(lo + lr, lo + lr + SUBLANES)
                for g in range(ng):
                    cs = slice(g * LANES, (g + 1) * LANES)
                    k, off = divmod(g * LANES, gw)
                    hh = _rglru_unit(pre[k][lr:lr + SUBLANES, off:off + LANES],
                                     pre[k][lr:lr + SUBLANES, gw + off:gw + off + LANES],
                                     xc[lr:lr + SUBLANES, cs],
                                     par_ref[0:8, cs], par_ref[8:16, cs], par_ref[16:24, cs],
                                     h[sb * ng + g])
                    h[sb * ng + g] = hh
                    hb_ref[rs, cs] = hh

    xcar_ref[...] = xr[0][0:halo, :]
    _store_state(hc_ref, h, nb)
    hl_ref[...] = hc_ref[...]


def _projscan(x, view, sc, sh, g, w, li, rp, h0, r, tm, mp, tn, xr_lo):
    n, d = x.size // view["d"], view["d"]
    din = w.shape[-1]
    dr = h0.shape[1]
    nt = n // tm
    const2 = lambda i: (0, 0)
    const3 = lambda i: (0, 0, 0)
    halo = (RNN_CONV_W - 1) * r
    return pl.pallas_call(
        functools.partial(_projscan_body, r=r, mp=mp, tn=tn, xr_lo=xr_lo,
                          x_batch_major=view["kind"] == "batch"),
        grid=(nt,),
        in_specs=[
            _x_spec(view, tm, order=lambda i: nt - 1 - i),
            pl.BlockSpec(sc.shape, const2),
            pl.BlockSpec(sh.shape, const2),
            pl.BlockSpec(g.shape, const2),
            pl.BlockSpec((None,) + w.shape[1:], lambda i: (li, 0, 0), pipeline_mode=pl.Buffered(1)),
            pl.BlockSpec(rp["cw"].shape, const2),
            pl.BlockSpec(rp["cb"].shape, const2),
            pl.BlockSpec(rp["wg"].shape, const3),
            pl.BlockSpec(rp["ba"].shape, const2),
            pl.BlockSpec(rp["bx"].shape, const2),
            pl.BlockSpec(rp["lam"].shape, const2),
            pl.BlockSpec(h0.shape, const2),
        ],
        out_specs=[
            pl.BlockSpec((tm, din), lambda i: (nt - 1 - i, 0)),
            pl.BlockSpec((tm, dr), lambda i: (nt - 1 - i, 0)),
            pl.BlockSpec(h0.shape, const2),
        ],
        out_shape=[
            jax.ShapeDtypeStruct((n, din), F32),
            jax.ShapeDtypeStruct((n, dr), F32),
            jax.ShapeDtypeStruct(h0.shape, F32),
        ],
        scratch_shapes=[
            pltpu.VMEM((tm, d), BF16),
            pltpu.VMEM((halo, dr), F32),
            pltpu.VMEM((3 * SUBLANES, dr), F32),
            pltpu.VMEM(h0.shape, F32),
        ],
        compiler_params=pltpu.CompilerParams(
            dimension_semantics=("arbitrary",),
            vmem_limit_bytes=56 * 1024 * 1024),
        name="proj_bwd_scan",
    )(_as_view(x, view), sc, sh, g, w, rp["cw"], rp["cb"], rp["wg"], rp["ba"], rp["bx"], rp["lam"], h0)


def _rows(ref, lo, n, batch_major=False):
    if len(ref.shape) == 2:
        return ref[lo:lo + n, :]
    if batch_major:
        r = ref.shape[0]
        return jnp.concatenate([ref[:, t, :] for t in range(lo // r, (lo + n) // r)], axis=0)
    r = ref.shape[1]
    return ref[lo // r:(lo + n) // r, :, :].reshape(n, ref.shape[2])


def _set_rows(ref, lo, n, val, batch_major=False):
    if len(ref.shape) == 2:
        ref[lo:lo + n, :] = val
    elif batch_major:
        r = ref.shape[0]
        for k, t in enumerate(range(lo // r, (lo + n) // r)):
            ref[:, t, :] = val[k * r:(k + 1) * r, :]
    else:
        r = ref.shape[1]
        ref[lo // r:(lo + n) // r, :, :] = val.reshape(n // r, r, ref.shape[2])


def _rglru_unit(pre_r, pre_i, xc, ba, bx, c8, h):
    gate_r = _sigmoid(pre_r + ba)
    gate_i = _sigmoid(pre_i + bx)
    a = jnp.exp(c8 * gate_r)
    s = 1.0 - a * a
    v = jnp.where(s > 0.0, s * lax.rsqrt(s), 0.0) * (gate_i * xc)
    return a * h + v


def _mix_body(bg_ref, cg_ref, xa_ref, xr_ref, gr_ref, cgh_ref, xah_ref, hb_ref, x_ref,
              cw_ref, cb_ref, wg_ref, ba_ref, bx_ref, lam_ref, h0_ref,
              caw_ref, wout_ref, gn_ref, gate_ref,
              o_ref, hl_ref,
              xcar_ref, zcar_ref, par_ref, hc_ref, y_ref, *, r, mp, x_batch_major):
    m, dr = xr_ref.shape
    dc = bg_ref.shape[1]
    nb = r // SUBLANES
    ng = dr // LANES
    nblk, gw, _ = wg_ref.shape
    halo = (RNN_CONV_W - 1) * r
    npieces = m // mp
    i = pl.program_id(0)
    last = pl.num_programs(0) - 1

    @pl.when(i == 0)
    def _init():
        xcar_ref[...] = jnp.zeros((halo, dr), F32)
        zcar_ref[...] = jnp.zeros((r, dc), F32)
        hc_ref[...] = h0_ref[...]
        _fill_rnn_params(par_ref, ba_ref, bx_ref, lam_ref)

    f32 = lambda ref, lo, n: ref[lo:lo + n, :].astype(F32)
    gate_rows = jnp.concatenate([gate_ref[...]] * (mp // MOD_ROWS), axis=0)
    h = list(_load_state(hc_ref, nb))
    for p in range(npieces):
        lo = p * mp
        if p == 0:
            xe = jnp.concatenate([xcar_ref[...], f32(xr_ref, 0, mp)], axis=0)
        else:
            xe = f32(xr_ref, lo - halo, mp + halo)
        xc = cb_ref[...] + cw_ref[0:1, :] * xe[0:mp, :]
        for k in range(1, RNN_CONV_W):
            xc = xc + cw_ref[k:k + 1, :] * xe[k * r:k * r + mp, :]
        pre = [jnp.dot(xc[:, k * gw:(k + 1) * gw].astype(BF16), wg_ref[k], preferred_element_type=F32)
               for k in range(nblk)]
        for t in range(mp // r):
            for sb in range(nb):
                lr = t * r + sb * SUBLANES
                rs = slice(lo + lr, lo + lr + SUBLANES)
                for g in range(ng):
                    cs = slice(g * LANES, (g + 1) * LANES)
                    k, off = divmod(g * LANES, gw)
                    hh = _rglru_unit(pre[k][lr:lr + SUBLANES, off:off + LANES],
                                     pre[k][lr:lr + SUBLANES, gw + off:gw + off + LANES],
                                     xc[lr:lr + SUBLANES, cs],
                                     par_ref[0:8, cs], par_ref[8:16, cs], par_ref[16:24, cs],
                                     h[sb * ng + g])
                    h[sb * ng + g] = hh
                    y_ref[rs, dc + g * LANES:dc + (g + 1) * LANES] = (
                        (hh + hb_ref[rs, cs]) * jax.nn.gelu(gr_ref[rs, cs].astype(F32)))

        z_prev = zcar_ref[...] if p == 0 else f32(cg_ref, lo - r, r) * f32(xa_ref, lo - r, r)
        if p == npieces - 1:
            z_next = jnp.where(i < last, f32(cgh_ref, 0, r) * f32(xah_ref, 0, r), 0.0)
        else:
            z_next = f32(cg_ref, lo + mp, r) * f32(xa_ref, lo + mp, r)
        ze = jnp.concatenate([z_prev, f32(cg_ref, lo, mp) * f32(xa_ref, lo, mp), z_next], axis=0)
        conv = (caw_ref[0:1, :] * ze[0:mp, :] + caw_ref[1:2, :] * ze[r:r + mp, :]
                + caw_ref[2:3, :] * ze[2 * r:2 * r + mp, :])
        y_ref[lo:lo + mp, 0:dc] = f32(bg_ref, lo, mp) * conv

        yo = jnp.dot(y_ref[lo:lo + mp, :].astype(BF16), wout_ref[...], preferred_element_type=F32)
        _set_rows(o_ref, lo, mp, _rows(x_ref, lo, mp, x_batch_major) + gate_rows * _rms(yo, gn_ref[...]))

    xcar_ref[...] = f32(xr_ref, m - halo, halo)
    zcar_ref[...] = f32(cg_ref, m - r, r) * f32(xa_ref, m - r, r)
    _store_state(hc_ref, h, nb)
    hl_ref[...] = hc_ref[...]


def _mix(proj, hb, x, view, rp, h0, caw, wout, li, gn, gate, r, m, mp, cols):
    n, d = proj.shape[0], view["d"]
    dr = h0.shape[1]
    dc = caw.shape[1]
    dmix = wout.shape[-2]
    nt = n // m
    hr = r
    nh = n // hr
    const2 = lambda i: (0, 0)
    const3 = lambda i: (0, 0, 0)
    halo = (RNN_CONV_W - 1) * r
    col = lambda k: pl.BlockSpec((m, dc), lambda i: (i, k))
    nxt = lambda k: pl.BlockSpec((hr, dc), lambda i: (jnp.minimum((i + 1) * (m // hr), nh - 1), k))
    xspec = _x_spec(view, m)
    out_view = dict(kind="flat", d=d) if view["kind"] == "batch" else view
    out_struct = jax.ShapeDtypeStruct((n, d) if view["kind"] == "batch" else _as_view(x, view).shape, F32)
    return pl.pallas_call(
        functools.partial(_mix_body, r=r, mp=mp, x_batch_major=view["kind"] == "batch"),
        grid=(nt,),
        in_specs=[
            col(cols["bg"]), col(cols["cg"]), col(cols["xa"]), col(cols["xr"]), col(cols["gr"]),
            nxt(cols["cg"]), nxt(cols["xa"]),
            pl.BlockSpec((m, dr), lambda i: (i, 0)),
            xspec,
            pl.BlockSpec(rp["cw"].shape, const2),
            pl.BlockSpec(rp["cb"].shape, const2),
            pl.BlockSpec(rp["wg"].shape, const3),
            pl.BlockSpec(rp["ba"].shape, const2),
            pl.BlockSpec(rp["bx"].shape, const2),
            pl.BlockSpec(rp["lam"].shape, const2),
            pl.BlockSpec(h0.shape, const2),
            pl.BlockSpec(caw.shape, const2),
            pl.BlockSpec((None,) + wout.shape[1:], lambda i: (li, 0, 0), pipeline_mode=pl.Buffered(1)),
            pl.BlockSpec(gn.shape, const2),
            pl.BlockSpec(gate.shape, const2),
        ],
        out_specs=[
            _x_spec(out_view, m),
            pl.BlockSpec(h0.shape, const2),
        ],
        out_shape=[
            out_struct,
            jax.ShapeDtypeStruct(h0.shape, F32),
        ],
        scratch_shapes=[
            pltpu.VMEM((halo, dr), F32),
            pltpu.VMEM((r, dc), F32),
            pltpu.VMEM((3 * SUBLANES, dr), F32),
            pltpu.VMEM(h0.shape, F32),
            pltpu.VMEM((m, dmix), F32),
        ],
        compiler_params=pltpu.CompilerParams(
            dimension_semantics=("arbitrary",),
            vmem_limit_bytes=56 * 1024 * 1024),
        name="mix_fwd",
    )(proj, proj, proj, proj, proj, proj, proj, hb, _as_view(x, view),
      rp["cw"], rp["cb"], rp["wg"], rp["ba"], rp["bx"], rp["lam"], h0, caw, wout, gn, gate)


def _ffn_body(x_ref, xp_ref, xn_ref, sc_ref, sh_ref, gate_ref, gin_ref, gout_ref,
              wg_ref, wv_ref, cg_ref, cv_ref, wd_ref, o_ref, u_ref, acc_ref, *, r, ts, piece, out_batch_major):
    tm, d = x_ref.shape
    tf = wd_ref.shape[0]
    i = pl.program_id(0)
    j = pl.program_id(1)
    ni = pl.num_programs(0)
    nj = pl.num_programs(1)
    hal = HALO_ROWS
    rows_all = tm + 2 * r

    def conv(h, c_ref, cols):
        cw = c_ref[:, cols]
        h_prev = jnp.concatenate([h[tm:tm + r, :], h[0:tm - r, :]], axis=0)
        h_next = jnp.concatenate([h[r:tm, :], h[tm + r:tm + 2 * r, :]], axis=0)
        return cw[0:1, :] * h_prev + cw[1:2, :] * h[0:tm, :] + cw[2:3, :] * h_next

    def sub_chunk(k, hg, hv):
        cols = slice(ts * k, ts * (k + 1))
        a = (jax.nn.gelu(conv(hg, cg_ref, cols)) * conv(hv, cv_ref, cols)).astype(BF16)
        return jnp.dot(a, wd_ref[cols, :], preferred_element_type=F32)

    def up(u, k):
        cols = slice(ts * k, ts * (k + 1))
        return (jnp.dot(u, wg_ref[:, cols], preferred_element_type=F32),
                jnp.dot(u, wv_ref[:, cols], preferred_element_type=F32))

    def chunk(first):
        u = u_ref[...]
        acc = first
        for k in range(0 if first is None else 1, tf // ts):
            p = sub_chunk(k, *up(u, k))
            acc = p if acc is None else acc + p
        return acc

    @pl.when(j == 0)
    def _first_chunk():
        def modnorm(xq):
            n = xq.shape[0]
            return _rms(xq, gin_ref[...]) * (1.0 + sc_ref[0:n, :]) + sh_ref[0:n, :]

        u_halo = modnorm(jnp.concatenate([xp_ref[hal - r:hal, :], xn_ref[0:r, :]], axis=0))
        u_halo = jnp.concatenate([jnp.where(i > 0, u_halo[0:r, :], 0.0),
                                  jnp.where(i < ni - 1, u_halo[r:2 * r, :], 0.0)], axis=0).astype(BF16)
        hg, hv = [], []
        for lo in range(0, tm, piece):
            n = min(piece, tm - lo)
            parts = [modnorm(x_ref[q:q + hal, :]).astype(BF16) for q in range(lo, lo + n, hal)]
            if lo + n == tm:
                parts.append(u_halo)
                n += 2 * r
            u_p = jnp.concatenate(parts, axis=0)
            u_ref[lo:lo + n, :] = u_p
            g_p, v_p = up(u_p, 0)
            hg.append(g_p)
            hv.append(v_p)
        first = sub_chunk(0, jnp.concatenate(hg, axis=0), jnp.concatenate(hv, axis=0))
        acc_ref[...] = chunk(first)

    @pl.when(jnp.logical_and(j > 0, j < nj - 1))
    def _middle_chunk():
        acc_ref[...] += chunk(None)

    @pl.when(jnp.logical_and(j > 0, j == nj - 1))
    def _last_chunk():
        total = acc_ref[...] + chunk(None)
        gate = jnp.concatenate([gate_ref[...]] * (piece // MOD_ROWS), axis=0)
        for lo in range(0, tm, piece):
            rows = slice(lo, lo + piece)
            _set_rows(o_ref, lo, piece, x_ref[rows, :] + gate * _rms(total[rows, :], gout_ref[...]),
                      out_batch_major)


def _ffn(x, sc, sh, gate, gin, gout, wup, cw, wd, li, r, tm, tf, ts, piece, out_batch_major):
    n, d = x.shape
    dff = wd.shape[-2]
    assert wup.shape[1:] == (2 * dff // tf, d, tf)
    assert n % tm == 0 and dff % tf == 0 and tf % ts == 0 and tm % HALO_ROWS == 0
    assert dff // tf >= 2, "the first and last hidden chunk of a tile must be different grid steps"
    ni = n // tm
    nj = dff // tf
    nh = n // HALO_ROWS
    const = lambda i, j: (0, 0)
    return pl.pallas_call(
        functools.partial(_ffn_body, r=r, ts=ts, piece=piece, out_batch_major=out_batch_major),
        grid=(ni, nj),
        in_specs=[
            pl.BlockSpec((tm, d), lambda i, j: (i, 0)),
            pl.BlockSpec((HALO_ROWS, d), lambda i, j: (jnp.maximum(i * (tm // HALO_ROWS) - 1, 0), 0)),
            pl.BlockSpec((HALO_ROWS, d), lambda i, j: (jnp.minimum((i + 1) * (tm // HALO_ROWS), nh - 1), 0)),
            pl.BlockSpec(sc.shape, const),
            pl.BlockSpec(sh.shape, const),
            pl.BlockSpec(gate.shape, const),
            pl.BlockSpec(gin.shape, const),
            pl.BlockSpec(gout.shape, const),
            pl.BlockSpec((None, None, d, tf), lambda i, j: (li, j, 0, 0)),
            pl.BlockSpec((None, None, d, tf), lambda i, j: (li, nj + j, 0, 0)),
            pl.BlockSpec((FFN_CONV_W, tf), lambda i, j: (0, j)),
            pl.BlockSpec((FFN_CONV_W, tf), lambda i, j: (0, nj + j)),
            pl.BlockSpec((None, tf, d), lambda i, j: (li, j, 0)),
        ],
        out_specs=(pl.BlockSpec((r, tm // r, d), lambda i, j: (0, i, 0)) if out_batch_major
                   else pl.BlockSpec((tm, d), lambda i, j: (i, 0))),
        out_shape=jax.ShapeDtypeStruct((r, n // r, d) if out_batch_major else (n, d), F32),
        scratch_shapes=[
            pltpu.VMEM((tm + 2 * r, d), BF16),
            pltpu.VMEM((tm, d), F32),
        ],
        compiler_params=pltpu.CompilerParams(
            dimension_semantics=("arbitrary", "arbitrary"),
            vmem_limit_bytes=56 * 1024 * 1024),
        name="conv_ffn",
    )(x, x, x, sc, sh, gate, gin, gout, wup, wup, cw, cw, wd)


def _block_diag(w, gw):
    heads, hd, _ = w.shape
    hpb = gw // hd
    w = w.reshape(heads // hpb, hpb, hd, hd)
    eye = jnp.eye(hpb, dtype=w.dtype)
    return jnp.einsum("nhij,hg->nhigj", w, eye).reshape(heads // hpb, gw, gw)


def _plan(d, dr, dff, n_rows, r, grid_rows):
    seq_rows = grid_rows * r if grid_rows else n_rows
    return dict(
        proj_rows=min(256, seq_rows),
        mix_rows=min(256, seq_rows),
        ffn_rows=min(512, n_rows),
        ffn_cols=min(1024, dff),
        norm_piece=min(128, n_rows),
        mix_piece=min(128, seq_rows),
        ffn_sub=min(2 * MXU_WIDTH, dff),
        proj_cols=3 * (d - dr),
    )


def _tile_rows(mod, r, rows):
    return jnp.tile(mod, (rows // r, 1))


def _layer(x, r, grid_rows, mods, h0, lw, plan, first, last):
    d = x.shape[-1]
    n = x.size // d
    sh1, sc1, g1, sh2, sc2, g2 = [_tile_rows(m, r, MOD_ROWS) for m in jnp.split(mods, N_MOD, axis=-1)]
    if first:
        assert not grid_rows
        view = dict(kind="batch", r=r, d=d)
    elif grid_rows:
        view = dict(kind="grid", r=r, d=d, grid_rows=grid_rows)
    else:
        view = dict(kind="flat", d=d)
    ng = lw["norm_g"]
    dr = h0.shape[-1]
    proj, hb, last_b = _projscan(x, view, sc1, sh1, ng[0:1], lw["w_in"], lw["li"], lw["rnn"][1], h0[:, 1], r,
                                 plan["proj_rows"], plan["mix_piece"], plan["proj_cols"], lw["cols"]["xr"] * dr)
    x, last_f = _mix(proj, hb, x, view, lw["rnn"][0], h0[:, 0], lw["conv_a_w"], lw["w_out"], lw["li"],
                     ng[1:2], g1, r, plan["mix_rows"], plan["mix_piece"], lw["cols"])
    x = x.reshape(n, d)
    x = _ffn(x, sc2, sh2, g2, ng[2:3], ng[3:4], lw["ffn_up"], lw["ffn_conv_w"], lw["ffn_down"], lw["li"],
             r, plan["ffn_rows"], plan["ffn_cols"], plan["ffn_sub"], plan["norm_piece"], last)
    return x, jnp.stack([last_f, last_b], axis=1)


def _forward(x_prompt, x_sample, state_h, c, c_ctx, w_ada, b_ada, norm_g, w_in, conv_a_w,
             rnn_conv_w, rnn_conv_b, rnn_w_a, rnn_b_a, rnn_w_x, rnn_b_x, rnn_lam, w_out,
             ffn_up, ffn_conv_w, ffn_down, plan_overrides=None):
    b, s, d = x_prompt.shape
    bd, l, _ = x_sample.shape
    depth = w_ada.shape[0]
    dr = rnn_lam.shape[-1]
    dc = conv_a_w.shape[-1]
    dff = ffn_down.shape[1]
    grid_rows = l // GRID_W
    gw = min(MXU_WIDTH, dr)

    plan_p = _plan(d, dr, dff, s * b, b, 0)
    plan_s = _plan(d, dr, dff, l * bd, bd, grid_rows)
    for p in (plan_p, plan_s):
        p.update(plan_overrides or {})

    xp, xs = x_prompt, x_sample

    cc = jnp.concatenate([c, jnp.broadcast_to(c_ctx[None, :], (SUBLANES, d))], axis=0)
    mods = _modulation(cc, w_ada, b_ada, min(1024, N_MOD * d))

    h_zero = jnp.zeros((b, 2, dr), F32)
    w_in_b, w_out_b, ffn_up_b, ffn_down_b = (w.astype(BF16) for w in (w_in, w_out, ffn_up, ffn_down))
    tf = plan_s["ffn_cols"]
    assert plan_p["ffn_cols"] == tf
    ffn_up_b = jnp.transpose(ffn_up_b.reshape(depth, d, 2 * dff // tf, tf), (0, 2, 1, 3))
    states = []
    for li in range(depth):
        rnn = []
        for di in range(2):
            wg = jnp.concatenate([_block_diag(rnn_w_a[li, di], gw), _block_diag(rnn_w_x[li, di], gw)],
                                 axis=-1).astype(BF16)
            rnn.append(dict(cw=rnn_conv_w[li, di], cb=rnn_conv_b[li, di][None, :], wg=wg,
                            ba=rnn_b_a[li, di][None, :], bx=rnn_b_x[li, di][None, :],
                            lam=rnn_lam[li, di][None, :], xr_col=3 * dc // dr))
        lw = dict(
            norm_g=norm_g[li],
            li=li,
            w_in=w_in_b,
            conv_a_w=conv_a_w[li],
            rnn=rnn,
            w_out=w_out_b,
            ffn_up=ffn_up_b,
            ffn_conv_w=ffn_conv_w[li],
            ffn_down=ffn_down_b,
            cols=dict(bg=0, cg=1, xa=2, xr=3 * dc // dr, gr=3 * dc // dr + 1),
        )
        mod_ctx = jnp.broadcast_to(mods[li, bd:bd + 1], (b, N_MOD * d))
        first, last = li == 0, li == depth - 1
        xp, st = _layer(xp, b, 0, mod_ctx, h_zero, lw, plan_p, first, last)
        states.append(st)
        xs, _ = _layer(xs, bd, grid_rows if li % 2 == 1 else 0, mods[li, :bd], state_h[:, li], lw, plan_s,
                       first, last)

    return xp, xs, jnp.stack(states, axis=1)


def kernel(x_prompt, x_sample, state_h, c, c_ctx, w_ada, b_ada, norm_g, w_in, conv_a_w, rnn_conv_w,
           rnn_conv_b, rnn_w_a, rnn_b_a, rnn_w_x, rnn_b_x, rnn_lam, w_out, ffn_up, ffn_conv_w, ffn_down):
    return _forward(x_prompt, x_sample, state_h, c, c_ctx, w_ada, b_ada, norm_g, w_in, conv_a_w,
                    rnn_conv_w, rnn_conv_b, rnn_w_a, rnn_b_a, rnn_w_x, rnn_b_x, rnn_lam, w_out,
                    ffn_up, ffn_conv_w, ffn_down)
```

```python
import functools

import jax
import jax.numpy as jnp
from jax import lax
from jax.experimental import pallas as pl
from jax.experimental.pallas import tpu as pltpu

GRID_W = 64
N_RNN_HEADS = 16
RGLRU_C = 8.0
N_MOD = 6
EPS = 1e-6
RNN_CONV_W = 4
CONV_A_W = 3
FFN_CONV_W = 3

SUBLANES = 8
LANES = 128
MXU_WIDTH = 256
VMEM_BYTES = 64 * 1024 * 1024
HALO_ROWS = 16
MOD_ROWS = 32

F32 = jnp.float32
BF16 = jnp.bfloat16


def _rms(x, g):
    var = jnp.mean(x * x, axis=-1, keepdims=True)
    return x * lax.rsqrt(var + EPS) * g


def _sigmoid(x):
    return 0.5 * jnp.tanh(0.5 * x) + 0.5


def _mod_body(cc_ref, w_ref, b_ref, o_ref):
    s = cc_ref[...]
    s = s * jax.nn.sigmoid(s)
    o_ref[...] = jnp.dot(s.astype(BF16), w_ref[...].astype(BF16),
                         preferred_element_type=F32) + b_ref[...]


def _modulation(cc, w_ada, b_ada, tn):
    depth, d, n = w_ada.shape
    rows = cc.shape[0]
    return pl.pallas_call(
        _mod_body,
        grid=(depth, n // tn),
        in_specs=[
            pl.BlockSpec((rows, d), lambda l, j: (0, 0)),
            pl.BlockSpec((None, d, tn), lambda l, j: (l, 0, j)),
            pl.BlockSpec((None, 1, tn), lambda l, j: (l, 0, j)),
        ],
        out_specs=pl.BlockSpec((None, rows, tn), lambda l, j: (l, 0, j)),
        out_shape=jax.ShapeDtypeStruct((depth, rows, n), F32),
        compiler_params=pltpu.CompilerParams(
            dimension_semantics=("arbitrary", "arbitrary"),
            vmem_limit_bytes=40 * 1024 * 1024),
        name="adaln_modulation",
    )(cc, w_ada, b_ada.reshape(depth, 1, n))


def _x_spec(view, tile_rows, order=lambda i: i):
    if view["kind"] == "flat":
        return pl.BlockSpec((tile_rows, view["d"]), lambda i: (order(i), 0))
    r = view["r"]
    steps = tile_rows // r
    if view["kind"] == "batch":
        return pl.BlockSpec((r, steps, view["d"]), lambda i: (0, order(i), 0))
    tpc = view["grid_rows"] // steps
    return pl.BlockSpec((steps, None, r, view["d"]), lambda i: (order(i) % tpc, order(i) // tpc, 0, 0))


def _as_view(x, view):
    if view["kind"] in ("flat", "batch"):
        return x
    return x.reshape(view["grid_rows"], GRID_W, view["r"], view["d"])


def _fill_rnn_params(par_ref, ba_ref, bx_ref, lam_ref):
    dr = par_ref.shape[1]
    par_ref[0:8, :] = jnp.broadcast_to(ba_ref[...], (SUBLANES, dr))
    par_ref[8:16, :] = jnp.broadcast_to(bx_ref[...], (SUBLANES, dr))
    par_ref[16:24, :] = jnp.broadcast_to(-RGLRU_C * jax.nn.softplus(-lam_ref[...]), (SUBLANES, dr))


def _load_state(hc_ref, nb):
    ng = hc_ref.shape[1] // LANES
    return tuple(hc_ref[sb * SUBLANES:(sb + 1) * SUBLANES, g * LANES:(g + 1) * LANES]
                 for sb in range(nb) for g in range(ng))


def _store_state(hc_ref, h, nb):
    ng = hc_ref.shape[1] // LANES
    for sb in range(nb):
        for g in range(ng):
            hc_ref[sb * SUBLANES:(sb + 1) * SUBLANES, g * LANES:(g + 1) * LANES] = h[sb * ng + g]


def _projscan_body(x_ref, sc_ref, sh_ref, g_ref, w_ref, cw_ref, cb_ref, wg_ref, ba_ref, bx_ref, lam_ref, h0_ref,
                   o_ref, hb_ref, hl_ref,
                   u_ref, xcar_ref, par_ref, hc_ref, *, r, mp, tn, xr_lo, x_batch_major):
    tm, d = u_ref.shape
    dr = hb_ref.shape[1]
    din = o_ref.shape[1]
    nb = r // SUBLANES
    ng = dr // LANES
    nblk, gw, _ = wg_ref.shape
    halo = (RNN_CONV_W - 1) * r
    npieces = tm // mp

    @pl.when(pl.program_id(0) == 0)
    def _init():
        xcar_ref[...] = jnp.zeros((halo, dr), F32)
        hc_ref[...] = h0_ref[...]
        _fill_rnn_params(par_ref, ba_ref, bx_ref, lam_ref)

    sc = jnp.concatenate([sc_ref[...]] * (mp // MOD_ROWS), axis=0)
    sh = jnp.concatenate([sh_ref[...]] * (mp // MOD_ROWS), axis=0)
    xr = []
    for p in range(npieces):
        lo = p * mp
        u = (_rms(_rows(x_ref, lo, mp, x_batch_major), g_ref[...]) * (1.0 + sc) + sh).astype(BF16)
        u_ref[lo:lo + mp, :] = u
        xr_p = jnp.dot(u, w_ref[:, xr_lo:xr_lo + dr], preferred_element_type=F32)
        o_ref[lo:lo + mp, xr_lo:xr_lo + dr] = xr_p
        xr.append(xr_p)

    def project(lo_c, hi_c):
        for lo in range(lo_c, hi_c, tn):
            hi = min(lo + tn, hi_c)
            o_ref[:, lo:hi] = jnp.dot(u_ref[...], w_ref[:, lo:hi], preferred_element_type=F32)

    xcs, pres = [], []
    for p in range(npieces):
        later = xcar_ref[...] if p == npieces - 1 else xr[p + 1][0:halo, :]
        xe = jnp.concatenate([xr[p], later], axis=0)
        xc = cb_ref[...] + cw_ref[RNN_CONV_W - 1:RNN_CONV_W, :] * xe[0:mp, :]
        for k in range(RNN_CONV_W - 1):
            off = (RNN_CONV_W - 1 - k) * r
            xc = xc + cw_ref[k:k + 1, :] * xe[off:off + mp, :]
        xcs.append(xc)
        pres.append([jnp.dot(xc[:, k * gw:(k + 1) * gw].astype(BF16), wg_ref[k], preferred_element_type=F32)
                     for k in range(nblk)])

    project(0, xr_lo)
    project(xr_lo + dr, din)

    h = list(_load_state(hc_ref, nb))
    for p in reversed(range(npieces)):
        lo = p * mp
        xc, pre = xcs[p], pres[p]
        for t in reversed(range(mp // r)):
            for sb in range(nb):
                lr = t * r + sb * SUBLANES
                rs = slice(lo + lr, lo + lr + SUBLANES)
                for g in range(ng):
                    cs = slice(g * LANES, (g + 1) * LANES)
                    k, off = divmod(g * LANES, gw)
                    hh = _rglru_unit(pre[k][lr:lr + SUBLANES, off:off + LANES],
                                     pre[k][lr:lr + SUBLANES, gw + off:gw + off + LANES],
                                     xc[lr:lr + SUBLANES, cs],
                                     par_ref[0:8, cs], par_ref[8:16, cs], par_ref[16:24, cs],
                                     h[sb * ng + g])
                    h[sb * ng + g] = hh
                    hb_ref[rs, cs] = hh

    xcar_ref[...] = xr[0][0:halo, :]
    _store_state(hc_ref, h, nb)
    hl_ref[...] = hc_ref[...]


def _projscan(x, view, sc, sh, g, w, li, rp, h0, r, tm, mp, tn, xr_lo):
    n, d = x.size // view["d"], view["d"]
    din = w.shape[-1]
    dr = h0.shape[1]
    nt = n // tm
    const2 = lambda i: (0, 0)
    const3 = lambda i: (0, 0, 0)
    halo = (RNN_CONV_W - 1) * r
    return pl.pallas_call(
        functools.partial(_projscan_body, r=r, mp=mp, tn=tn, xr_lo=xr_lo,
                          x_batch_major=view["kind"] == "batch"),
        grid=(nt,),
        in_specs=[
            _x_spec(view, tm, order=lambda i: nt - 1 - i),
            pl.BlockSpec(sc.shape, const2),
            pl.BlockSpec(sh.shape, const2),
            pl.BlockSpec(g.shape, const2),
            pl.BlockSpec((None,) + w.shape[1:], lambda i: (li, 0, 0), pipeline_mode=pl.Buffered(1)),
            pl.BlockSpec(rp["cw"].shape, const2),
            pl.BlockSpec(rp["cb"].shape, const2),
            pl.BlockSpec(rp["wg"].shape, const3),
            pl.BlockSpec(rp["ba"].shape, const2),
            pl.BlockSpec(rp["bx"].shape, const2),
            pl.BlockSpec(rp["lam"].shape, const2),
            pl.BlockSpec(h0.shape, const2),
        ],
        out_specs=[
            pl.BlockSpec((tm, din), lambda i: (nt - 1 - i, 0)),
            pl.BlockSpec((tm, dr), lambda i: (nt - 1 - i, 0)),
            pl.BlockSpec(h0.shape, const2),
        ],
        out_shape=[
            jax.ShapeDtypeStruct((n, din), F32),
            jax.ShapeDtypeStruct((n, dr), F32),
            jax.ShapeDtypeStruct(h0.shape, F32),
        ],
        scratch_shapes=[
            pltpu.VMEM((tm, d), BF16),
            pltpu.VMEM((halo, dr), F32),
            pltpu.VMEM((3 * SUBLANES, dr), F32),
            pltpu.VMEM(h0.shape, F32),
        ],
        compiler_params=pltpu.CompilerParams(
            dimension_semantics=("arbitrary",),
            vmem_limit_bytes=60 * 1024 * 1024),
        name="proj_bwd_scan",
    )(_as_view(x, view), sc, sh, g, w, rp["cw"], rp["cb"], rp["wg"], rp["ba"], rp["bx"], rp["lam"], h0)


def _rows(ref, lo, n, batch_major=False):
    if len(ref.shape) == 2:
        return ref[lo:lo + n, :]
    if batch_major:
        r = ref.shape[0]
        return jnp.concatenate([ref[:, t, :] for t in range(lo // r, (lo + n) // r)], axis=0)
    r = ref.shape[1]
    return ref[lo // r:(lo + n) // r, :, :].reshape(n, ref.shape[2])


def _set_rows(ref, lo, n, val, batch_major=False):
    if len(ref.shape) == 2:
        ref[lo:lo + n, :] = val
    elif batch_major:
        r = ref.shape[0]
        for k, t in enumerate(range(lo // r, (lo + n) // r)):
            ref[:, t, :] = val[k * r:(k + 1) * r, :]
    else:
        r = ref.shape[1]
        ref[lo // r:(lo + n) // r, :, :] = val.reshape(n // r, r, ref.shape[2])


def _rglru_unit(pre_r, pre_i, xc, ba, bx, c8, h):
    gate_r = _sigmoid(pre_r + ba)
    gate_i = _sigmoid(pre_i + bx)
    a = jnp.exp(c8 * gate_r)
    s = 1.0 - a * a
    v = jnp.where(s > 0.0, s * lax.rsqrt(s), 0.0) * (gate_i * xc)
    return a * h + v


def _mix_body(bg_ref, cg_ref, xa_ref, xr_ref, gr_ref, cgh_ref, xah_ref, hb_ref, x_ref,
              cw_ref, cb_ref, wg_ref, ba_ref, bx_ref, lam_ref, h0_ref,
              caw_ref, wout_ref, gn_ref, gate_ref,
              o_ref, hl_ref,
              xcar_ref, zcar_ref, par_ref, hc_ref, y_ref, *, r, mp, x_batch_major):
    m, dr = xr_ref.shape
    dc = bg_ref.shape[1]
    nb = r // SUBLANES
    ng = dr // LANES
    nblk, gw, _ = wg_ref.shape
    halo = (RNN_CONV_W - 1) * r
    npieces = m // mp
    i = pl.program_id(0)
    last = pl.num_programs(0) - 1

    @pl.when(i == 0)
    def _init():
        xcar_ref[...] = jnp.zeros((halo, dr), F32)
        zcar_ref[...] = jnp.zeros((r, dc), F32)
        hc_ref[...] = h0_ref[...]
        _fill_rnn_params(par_ref, ba_ref, bx_ref, lam_ref)

    f32 = lambda ref, lo, n: ref[lo:lo + n, :].astype(F32)
    gate_rows = jnp.concatenate([gate_ref[...]] * (mp // MOD_ROWS), axis=0)
    h = list(_load_state(hc_ref, nb))
    for p in range(npieces):
        lo = p * mp
        if p == 0:
            xe = jnp.concatenate([xcar_ref[...], f32(xr_ref, 0, mp)], axis=0)
        else:
            xe = f32(xr_ref, lo - halo, mp + halo)
        xc = cb_ref[...] + cw_ref[0:1, :] * xe[0:mp, :]
        for k in range(1, RNN_CONV_W):
            xc = xc + cw_ref[k:k + 1, :] * xe[k * r:k * r + mp, :]
        pre = [jnp.dot(xc[:, k * gw:(k + 1) * gw].astype(BF16), wg_ref[k], preferred_element_type=F32)
               for k in range(nblk)]
        for t in range(mp // r):
            for sb in range(nb):
                lr = t * r + sb * SUBLANES
                rs = slice(lo + lr, lo + lr + SUBLANES)
                for g in range(ng):
                    cs = slice(g * LANES, (g + 1) * LANES)
                    k, off = divmod(g * LANES, gw)
                    hh = _rglru_unit(pre[k][lr:lr + SUBLANES, off:off + LANES],
                                     pre[k][lr:lr + SUBLANES, gw + off:gw + off + LANES],
                                     xc[lr:lr + SUBLANES, cs],
                                     par_ref[0:8, cs], par_ref[8:16, cs], par_ref[16:24, cs],
                                     h[sb * ng + g])
                    h[sb * ng + g] = hh
                    y_ref[rs, dc + g * LANES:dc + (g + 1) * LANES] = (
                        (hh + hb_ref[rs, cs]) * jax.nn.gelu(gr_ref[rs, cs].astype(F32)))

        z_prev = zcar_ref[...] if p == 0 else f32(cg_ref, lo - r, r) * f32(xa_ref, lo - r, r)
        if p == npieces - 1:
            z_next = jnp.where(i < last, f32(cgh_ref, 0, r) * f32(xah_ref, 0, r), 0.0)
        else:
            z_next = f32(cg_ref, lo + mp, r) * f32(xa_ref, lo + mp, r)
        ze = jnp.concatenate([z_prev, f32(cg_ref, lo, mp) * f32(xa_ref, lo, mp), z_next], axis=0)
        conv = (caw_ref[0:1, :] * ze[0:mp, :] + caw_ref[1:2, :] * ze[r:r + mp, :]
                + caw_ref[2:3, :] * ze[2 * r:2 * r + mp, :])
        y_ref[lo:lo + mp, 0:dc] = f32(bg_ref, lo, mp) * conv

        yo = jnp.dot(y_ref[lo:lo + mp, :].astype(BF16), wout_ref[...], preferred_element_type=F32)
        _set_rows(o_ref, lo, mp, _rows(x_ref, lo, mp, x_batch_major) + gate_rows * _rms(yo, gn_ref[...]))

    xcar_ref[...] = f32(xr_ref, m - halo, halo)
    zcar_ref[...] = f32(cg_ref, m - r, r) * f32(xa_ref, m - r, r)
    _store_state(hc_ref, h, nb)
    hl_ref[...] = hc_ref[...]


def _mix(proj, hb, x, view, rp, h0, caw, wout, li, gn, gate, r, m, mp, cols):
    n, d = proj.shape[0], view["d"]
    dr = h0.shape[1]
    dc = caw.shape[1]
    dmix = wout.shape[-2]
    nt = n // m
    hr = r
    nh = n // hr
    const2 = lambda i: (0, 0)
    const3 = lambda i: (0, 0, 0)
    halo = (RNN_CONV_W - 1) * r
    col = lambda k: pl.BlockSpec((m, dc), lambda i: (i, k))
    nxt = lambda k: pl.BlockSpec((hr, dc), lambda i: (jnp.minimum((i + 1) * (m // hr), nh - 1), k))
    xspec = _x_spec(view, m)
    out_view = dict(kind="flat", d=d) if view["kind"] == "batch" else view
    out_struct = jax.ShapeDtypeStruct((n, d) if view["kind"] == "batch" else _as_view(x, view).shape, F32)
    return pl.pallas_call(
        functools.partial(_mix_body, r=r, mp=mp, x_batch_major=view["kind"] == "batch"),
        grid=(nt,),
        in_specs=[
            col(cols["bg"]), col(cols["cg"]), col(cols["xa"]), col(cols["xr"]), col(cols["gr"]),
            nxt(cols["cg"]), nxt(cols["xa"]),
            pl.BlockSpec((m, dr), lambda i: (i, 0)),
            xspec,
            pl.BlockSpec(rp["cw"].shape, const2),
            pl.BlockSpec(rp["cb"].shape, const2),
            pl.BlockSpec(rp["wg"].shape, const3),
            pl.BlockSpec(rp["ba"].shape, const2),
            pl.BlockSpec(rp["bx"].shape, const2),
            pl.BlockSpec(rp["lam"].shape, const2),
            pl.BlockSpec(h0.shape, const2),
            pl.BlockSpec(caw.shape, const2),
            pl.BlockSpec((None,) + wout.shape[1:], lambda i: (li, 0, 0), pipeline_mode=pl.Buffered(1)),
            pl.BlockSpec(gn.shape, const2),
            pl.BlockSpec(gate.shape, const2),
        ],
        out_specs=[
            _x_spec(out_view, m),
            pl.BlockSpec(h0.shape, const2),
        ],
        out_shape=[
            out_struct,
            jax.ShapeDtypeStruct(h0.shape, F32),
        ],
        scratch_shapes=[
            pltpu.VMEM((halo, dr), F32),
            pltpu.VMEM((r, dc), F32),
            pltpu.VMEM((3 * SUBLANES, dr), F32),
            pltpu.VMEM(h0.shape, F32),
            pltpu.VMEM((m, dmix), F32),
        ],
        compiler_params=pltpu.CompilerParams(
            dimension_semantics=("arbitrary",),
            vmem_limit_bytes=56 * 1024 * 1024),
        name="mix_fwd",
    )(proj, proj, proj, proj, proj, proj, proj, hb, _as_view(x, view),
      rp["cw"], rp["cb"], rp["wg"], rp["ba"], rp["bx"], rp["lam"], h0, caw, wout, gn, gate)


def _ffn_body(x_ref, xp_ref, xn_ref, sc_ref, sh_ref, gate_ref, gin_ref, gout_ref,
              wg_ref, wv_ref, cg_ref, cv_ref, wd_ref, o_ref, u_ref, acc_ref, *, r, ts, piece, out_batch_major):
    tm, d = x_ref.shape
    tf = wd_ref.shape[0]
    i = pl.program_id(0)
    j = pl.program_id(1)
    ni = pl.num_programs(0)
    nj = pl.num_programs(1)
    hal = HALO_ROWS
    rows_all = tm + 2 * r

    def conv(h, c_ref, cols):
        cw = c_ref[:, cols]
        h_prev = jnp.concatenate([h[tm:tm + r, :], h[0:tm - r, :]], axis=0)
        h_next = jnp.concatenate([h[r:tm, :], h[tm + r:tm + 2 * r, :]], axis=0)
        return cw[0:1, :] * h_prev + cw[1:2, :] * h[0:tm, :] + cw[2:3, :] * h_next

    def sub_chunk(k, hg, hv):
        cols = slice(ts * k, ts * (k + 1))
        a = (jax.nn.gelu(conv(hg, cg_ref, cols)) * conv(hv, cv_ref, cols)).astype(BF16)
        return jnp.dot(a, wd_ref[cols, :], preferred_element_type=F32)

    def up(u, k):
        cols = slice(ts * k, ts * (k + 1))
        return (jnp.dot(u, wg_ref[:, cols], preferred_element_type=F32),
                jnp.dot(u, wv_ref[:, cols], preferred_element_type=F32))

    def chunk(first):
        u = u_ref[...]
        acc = first
        for k in range(0 if first is None else 1, tf // ts):
            p = sub_chunk(k, *up(u, k))
            acc = p if acc is None else acc + p
        return acc

    @pl.when(j == 0)
    def _first_chunk():
        def modnorm(xq):
            n = xq.shape[0]
            return _rms(xq, gin_ref[...]) * (1.0 + sc_ref[0:n, :]) + sh_ref[0:n, :]

        u_halo = modnorm(jnp.concatenate([xp_ref[hal - r:hal, :], xn_ref[0:r, :]], axis=0))
        u_halo = jnp.concatenate([jnp.where(i > 0, u_halo[0:r, :], 0.0),
                                  jnp.where(i < ni - 1, u_halo[r:2 * r, :], 0.0)], axis=0).astype(BF16)
        hg, hv = [], []
        for lo in range(0, tm, piece):
            n = min(piece, tm - lo)
            parts = [modnorm(x_ref[q:q + hal, :]).astype(BF16) for q in range(lo, lo + n, hal)]
            if lo + n == tm:
                parts.append(u_halo)
                n += 2 * r
            u_p = jnp.concatenate(parts, axis=0)
            u_ref[lo:lo + n, :] = u_p
            g_p, v_p = up(u_p, 0)
            hg.append(g_p)
            hv.append(v_p)
        first = sub_chunk(0, jnp.concatenate(hg, axis=0), jnp.concatenate(hv, axis=0))
        acc_ref[...] = chunk(first)

    @pl.when(jnp.logical_and(j > 0, j < nj - 1))
    def _middle_chunk():
        acc_ref[...] += chunk(None)

    @pl.when(jnp.logical_and(j > 0, j == nj - 1))
    def _last_chunk():
        total = acc_ref[...] + chunk(None)
        gate = jnp.concatenate([gate_ref[...]] * (piece // MOD_ROWS), axis=0)
        for lo in range(0, tm, piece):
            rows = slice(lo, lo + piece)
            _set_rows(o_ref, lo, piece, x_ref[rows, :] + gate * _rms(total[rows, :], gout_ref[...]),
                      out_batch_major)


def _ffn(x, sc, sh, gate, gin, gout, wup, cw, wd, li, r, tm, tf, ts, piece, out_batch_major):
    n, d = x.shape
    dff = wd.shape[-2]
    assert n % tm == 0 and dff % tf == 0 and tf % ts == 0 and tm % HALO_ROWS == 0
    assert dff // tf >= 2, "the first and last hidden chunk of a tile must be different grid steps"
    ni = n // tm
    nj = dff // tf
    nh = n // HALO_ROWS
    const = lambda i, j: (0, 0)
    return pl.pallas_call(
        functools.partial(_ffn_body, r=r, ts=ts, piece=piece, out_batch_major=out_batch_major),
        grid=(ni, nj),
        in_specs=[
            pl.BlockSpec((tm, d), lambda i, j: (i, 0)),
            pl.BlockSpec((HALO_ROWS, d), lambda i, j: (jnp.maximum(i * (tm // HALO_ROWS) - 1, 0), 0)),
            pl.BlockSpec((HALO_ROWS, d), lambda i, j: (jnp.minimum((i + 1) * (tm // HALO_ROWS), nh - 1), 0)),
            pl.BlockSpec(sc.shape, const),
            pl.BlockSpec(sh.shape, const),
            pl.BlockSpec(gate.shape, const),
            pl.BlockSpec(gin.shape, const),
            pl.BlockSpec(gout.shape, const),
            pl.BlockSpec((None, d, tf), lambda i, j: (li, 0, j)),
            pl.BlockSpec((None, d, tf), lambda i, j: (li, 0, nj + j)),
            pl.BlockSpec((FFN_CONV_W, tf), lambda i, j: (0, j)),
            pl.BlockSpec((FFN_CONV_W, tf), lambda i, j: (0, nj + j)),
            pl.BlockSpec((None, tf, d), lambda i, j: (li, j, 0)),
        ],
        out_specs=(pl.BlockSpec((r, tm // r, d), lambda i, j: (0, i, 0)) if out_batch_major
                   else pl.BlockSpec((tm, d), lambda i, j: (i, 0))),
        out_shape=jax.ShapeDtypeStruct((r, n // r, d) if out_batch_major else (n, d), F32),
        scratch_shapes=[
            pltpu.VMEM((tm + 2 * r, d), BF16),
            pltpu.VMEM((tm, d), F32),
        ],
        compiler_params=pltpu.CompilerParams(
            dimension_semantics=("arbitrary", "arbitrary"),
            vmem_limit_bytes=56 * 1024 * 1024),
        name="conv_ffn",
    )(x, x, x, sc, sh, gate, gin, gout, wup, wup, cw, cw, wd)


def _block_diag(w, gw):
    heads, hd, _ = w.shape
    hpb = gw // hd
    w = w.reshape(heads // hpb, hpb, hd, hd)
    eye = jnp.eye(hpb, dtype=w.dtype)
    return jnp.einsum("nhij,hg->nhigj", w, eye).reshape(heads // hpb, gw, gw)


def _plan(d, dr, dff, n_rows, r, grid_rows):
    seq_rows = grid_rows * r if grid_rows else n_rows
    return dict(
        proj_rows=min(512, seq_rows),
        mix_rows=min(256, seq_rows),
        ffn_rows=min(512, n_rows),
        ffn_cols=min(1024, dff),
        norm_piece=min(128, n_rows),
        mix_piece=min(128, seq_rows),
        ffn_sub=min(2 * MXU_WIDTH, dff),
        proj_cols=3 * (d - dr),
    )


def _tile_rows(mod, r, rows):
    return jnp.tile(mod, (rows // r, 1))


def _layer(x, r, grid_rows, mods, h0, lw, plan, first, last):
    d = x.shape[-1]
    n = x.size // d
    sh1, sc1, g1, sh2, sc2, g2 = [_tile_rows(m, r, MOD_ROWS) for m in jnp.split(mods, N_MOD, axis=-1)]
    if first:
        assert not grid_rows
        view = dict(kind="batch", r=r, d=d)
    elif grid_rows:
        view = dict(kind="grid", r=r, d=d, grid_rows=grid_rows)
    else:
        view = dict(kind="flat", d=d)
    ng = lw["norm_g"]
    dr = h0.shape[-1]
    proj, hb, last_b = _projscan(x, view, sc1, sh1, ng[0:1], lw["w_in"], lw["li"], lw["rnn"][1], h0[:, 1], r,
                                 plan["proj_rows"], plan["mix_piece"], plan["proj_cols"], lw["cols"]["xr"] * dr)
    x, last_f = _mix(proj, hb, x, view, lw["rnn"][0], h0[:, 0], lw["conv_a_w"], lw["w_out"], lw["li"],
                     ng[1:2], g1, r, plan["mix_rows"], plan["mix_piece"], lw["cols"])
    x = x.reshape(n, d)
    x = _ffn(x, sc2, sh2, g2, ng[2:3], ng[3:4], lw["ffn_up"], lw["ffn_conv_w"], lw["ffn_down"], lw["li"],
             r, plan["ffn_rows"], plan["ffn_cols"], plan["ffn_sub"], plan["norm_piece"], last)
    return x, jnp.stack([last_f, last_b], axis=1)


def _forward(x_prompt, x_sample, state_h, c, c_ctx, w_ada, b_ada, norm_g, w_in, conv_a_w,
             rnn_conv_w, rnn_conv_b, rnn_w_a, rnn_b_a, rnn_w_x, rnn_b_x, rnn_lam, w_out,
             ffn_up, ffn_conv_w, ffn_down, plan_overrides=None):
    b, s, d = x_prompt.shape
    bd, l, _ = x_sample.shape
    depth = w_ada.shape[0]
    dr = rnn_lam.shape[-1]
    dc = conv_a_w.shape[-1]
    dff = ffn_down.shape[1]
    grid_rows = l // GRID_W
    gw = min(MXU_WIDTH, dr)

    plan_p = _plan(d, dr, dff, s * b, b, 0)
    plan_s = _plan(d, dr, dff, l * bd, bd, grid_rows)
    for p in (plan_p, plan_s):
        p.update(plan_overrides or {})

    xp, xs = x_prompt, x_sample

    cc = jnp.concatenate([c, jnp.broadcast_to(c_ctx[None, :], (SUBLANES, d))], axis=0)
    mods = _modulation(cc, w_ada, b_ada, min(1024, N_MOD * d))

    h_zero = jnp.zeros((b, 2, dr), F32)
    w_in_b, w_out_b, ffn_up_b, ffn_down_b = (w.astype(BF16) for w in (w_in, w_out, ffn_up, ffn_down))
    states = []
    for li in range(depth):
        rnn = []
        for di in range(2):
            wg = jnp.concatenate([_block_diag(rnn_w_a[li, di], gw), _block_diag(rnn_w_x[li, di], gw)],
                                 axis=-1).astype(BF16)
            rnn.append(dict(cw=rnn_conv_w[li, di], cb=rnn_conv_b[li, di][None, :], wg=wg,
                            ba=rnn_b_a[li, di][None, :], bx=rnn_b_x[li, di][None, :],
                            lam=rnn_lam[li, di][None, :], xr_col=3 * dc // dr))
        lw = dict(
            norm_g=norm_g[li],
            li=li,
            w_in=w_in_b,
            conv_a_w=conv_a_w[li],
            rnn=rnn,
            w_out=w_out_b,
            ffn_up=ffn_up_b,
            ffn_conv_w=ffn_conv_w[li],
            ffn_down=ffn_down_b,
            cols=dict(bg=0, cg=1, xa=2, xr=3 * dc // dr, gr=3 * dc // dr + 1),
        )
        mod_ctx = jnp.broadcast_to(mods[li, bd:bd + 1], (b, N_MOD * d))
        first, last = li == 0, li == depth - 1
        xp, st = _layer(xp, b, 0, mod_ctx, h_zero, lw, plan_p, first, last)
        states.append(st)
        xs, _ = _layer(xs, bd, grid_rows if li % 2 == 1 else 0, mods[li, :bd], state_h[:, li], lw, plan_s,
                       first, last)

    return xp, xs, jnp.stack(states, axis=1)


def kernel(x_prompt, x_sample, state_h, c, c_ctx, w_ada, b_ada, norm_g, w_in, conv_a_w, rnn_conv_w,
           rnn_conv_b, rnn_w_a, rnn_b_a, rnn_w_x, rnn_b_x, rnn_lam, w_out, ffn_up, ffn_conv_w, ffn_down):
    return _forward(x_prompt, x_sample, state_h, c, c_ctx, w_ada, b_ada, norm_g, w_in, conv_a_w,
                    rnn_conv_w, rnn_conv_b, rnn_w_a, rnn_b_a, rnn_w_x, rnn_b_x, rnn_lam, w_out,
                    ffn_up, ffn_conv_w, ffn_down)
```

```python
import functools

import jax
import jax.numpy as jnp
from jax import lax
from jax.experimental import pallas as pl
from jax.experimental.pallas import tpu as pltpu

GRID_W = 64
N_RNN_HEADS = 16
RGLRU_C = 8.0
N_MOD = 6
EPS = 1e-6
RNN_CONV_W = 4
CONV_A_W = 3
FFN_CONV_W = 3

SUBLANES = 8
LANES = 128
MXU_WIDTH = 256
VMEM_BYTES = 64 * 1024 * 1024
HALO_ROWS = 16
MOD_ROWS = 32

F32 = jnp.float32
BF16 = jnp.bfloat16


def _rms(x, g):
    var = jnp.mean(x * x, axis=-1, keepdims=True)
    return x * lax.rsqrt(var + EPS) * g


def _sigmoid(x):
    return 0.5 * jnp.tanh(0.5 * x) + 0.5


def _mod_body(cc_ref, w_ref, b_ref, o_ref):
    s = cc_ref[...]
    s = s * jax.nn.sigmoid(s)
    o_ref[...] = jnp.dot(s.astype(BF16), w_ref[...].astype(BF16),
                         preferred_element_type=F32) + b_ref[...]


def _modulation(cc, w_ada, b_ada, tn):
    depth, d, n = w_ada.shape
    rows = cc.shape[0]
    return pl.pallas_call(
        _mod_body,
        grid=(depth, n // tn),
        in_specs=[
            pl.BlockSpec((rows, d), lambda l, j: (0, 0)),
            pl.BlockSpec((None, d, tn), lambda l, j: (l, 0, j)),
            pl.BlockSpec((None, 1, tn), lambda l, j: (l, 0, j)),
        ],
        out_specs=pl.BlockSpec((None, rows, tn), lambda l, j: (l, 0, j)),
        out_shape=jax.ShapeDtypeStruct((depth, rows, n), F32),
        compiler_params=pltpu.CompilerParams(
            dimension_semantics=("arbitrary", "arbitrary"),
            vmem_limit_bytes=40 * 1024 * 1024),
        name="adaln_modulation",
    )(cc, w_ada, b_ada.reshape(depth, 1, n))


def _x_spec(view, tile_rows, order=lambda i: i):
    if view["kind"] == "flat":
        return pl.BlockSpec((tile_rows, view["d"]), lambda i: (order(i), 0))
    r = view["r"]
    steps = tile_rows // r
    if view["kind"] == "batch":
        return pl.BlockSpec((r, steps, view["d"]), lambda i: (0, order(i), 0))
    tpc = view["grid_rows"] // steps
    return pl.BlockSpec((steps, None, r, view["d"]), lambda i: (order(i) % tpc, order(i) // tpc, 0, 0))


def _as_view(x, view):
    if view["kind"] in ("flat", "batch"):
        return x
    return x.reshape(view["grid_rows"], GRID_W, view["r"], view["d"])


def _fill_rnn_params(par_ref, ba_ref, bx_ref, lam_ref):
    dr = par_ref.shape[1]
    par_ref[0:8, :] = jnp.broadcast_to(ba_ref[...], (SUBLANES, dr))
    par_ref[8:16, :] = jnp.broadcast_to(bx_ref[...], (SUBLANES, dr))
    par_ref[16:24, :] = jnp.broadcast_to(-RGLRU_C * jax.nn.softplus(-lam_ref[...]), (SUBLANES, dr))


def _load_state(hc_ref, nb):
    ng = hc_ref.shape[1] // LANES
    return tuple(hc_ref[sb * SUBLANES:(sb + 1) * SUBLANES, g * LANES:(g + 1) * LANES]
                 for sb in range(nb) for g in range(ng))


def _store_state(hc_ref, h, nb):
    ng = hc_ref.shape[1] // LANES
    for sb in range(nb):
        for g in range(ng):
            hc_ref[sb * SUBLANES:(sb + 1) * SUBLANES, g * LANES:(g + 1) * LANES] = h[sb * ng + g]


def _projscan_body(x_ref, sc_ref, sh_ref, g_ref, w_ref, cw_ref, cb_ref, wg_ref, ba_ref, bx_ref, lam_ref, h0_ref,
                   o_ref, hb_ref, hl_ref,
                   u_ref, xcar_ref, par_ref, hc_ref, *, r, mp, tn, xr_lo, x_batch_major):
    tm, d = u_ref.shape
    dr = hb_ref.shape[1]
    din = o_ref.shape[1]
    nb = r // SUBLANES
    ng = dr // LANES
    nblk, gw, _ = wg_ref.shape
    halo = (RNN_CONV_W - 1) * r
    npieces = tm // mp

    @pl.when(pl.program_id(0) == 0)
    def _init():
        xcar_ref[...] = jnp.zeros((halo, dr), F32)
        hc_ref[...] = h0_ref[...]
        _fill_rnn_params(par_ref, ba_ref, bx_ref, lam_ref)

    sc = jnp.concatenate([sc_ref[...]] * (mp // MOD_ROWS), axis=0)
    sh = jnp.concatenate([sh_ref[...]] * (mp // MOD_ROWS), axis=0)
    xr = []
    for p in range(npieces):
        lo = p * mp
        u = (_rms(_rows(x_ref, lo, mp, x_batch_major), g_ref[...]) * (1.0 + sc) + sh).astype(BF16)
        u_ref[lo:lo + mp, :] = u
        xr_p = jnp.dot(u, w_ref[:, xr_lo:xr_lo + dr], preferred_element_type=F32)
        o_ref[lo:lo + mp, xr_lo:xr_lo + dr] = xr_p
        xr.append(xr_p)

    def project(lo_c, hi_c):
        for lo in range(lo_c, hi_c, tn):
            hi = min(lo + tn, hi_c)
            o_ref[:, lo:hi] = jnp.dot(u_ref[...], w_ref[:, lo:hi], preferred_element_type=F32)

    xcs, pres = [], []
    for p in range(npieces):
        later = xcar_ref[...] if p == npieces - 1 else xr[p + 1][0:halo, :]
        xe = jnp.concatenate([xr[p], later], axis=0)
        xc = cb_ref[...] + cw_ref[RNN_CONV_W - 1:RNN_CONV_W, :] * xe[0:mp, :]
        for k in range(RNN_CONV_W - 1):
            off = (RNN_CONV_W - 1 - k) * r
            xc = xc + cw_ref[k:k + 1, :] * xe[off:off + mp, :]
        xcs.append(xc)
        pres.append([jnp.dot(xc[:, k * gw:(k + 1) * gw].astype(BF16), wg_ref[k], preferred_element_type=F32)
                     for k in range(nblk)])

    project(0, xr_lo)
    project(xr_lo + dr, din)

    h = list(_load_state(hc_ref, nb))
    for p in reversed(range(npieces)):
        lo = p * mp
        xc, pre = xcs[p], pres[p]
        for t in reversed(range(mp // r)):
            for sb in range(nb):
                lr = t * r + sb * SUBLANES
                rs = slice(lo + lr, lo + lr + SUBLANES)
                for g in range(ng):
                    cs = slice(g * LANES, (g + 1) * LANES)
                    k, off = divmod(g * LANES, gw)
                    hh = _rglru_unit(pre[k][lr:lr + SUBLANES, off:off + LANES],
                                     pre[k][lr:lr + SUBLANES, gw + off:gw + off + LANES],
                                     xc[lr:lr + SUBLANES, cs],
                                     par_ref[0:8, cs], par_ref[8:16, cs], par_ref[16:24, cs],
                                     h[sb * ng + g])
                    h[sb * ng + g] = hh
                    hb_ref[rs, cs] = hh

    xcar_ref[...] = xr[0][0:halo, :]
    _store_state(hc_ref, h, nb)
    hl_ref[...] = hc_ref[...]


def _projscan(x, view, sc, sh, g, w, li, rp, h0, r, tm, mp, tn, xr_lo):
    n, d = x.size // view["d"], view["d"]
    din = w.shape[-1]
    dr = h0.shape[1]
    nt = n // tm
    const2 = lambda i: (0, 0)
    const3 = lambda i: (0, 0, 0)
    halo = (RNN_CONV_W - 1) * r
    return pl.pallas_call(
        functools.partial(_projscan_body, r=r, mp=mp, tn=tn, xr_lo=xr_lo,
                          x_batch_major=view["kind"] == "batch"),
        grid=(nt,),
        in_specs=[
            _x_spec(view, tm, order=lambda i: nt - 1 - i),
            pl.BlockSpec(sc.shape, const2),
            pl.BlockSpec(sh.shape, const2),
            pl.BlockSpec(g.shape, const2),
            pl.BlockSpec((None,) + w.shape[1:], lambda i: (li, 0, 0), pipeline_mode=pl.Buffered(1)),
            pl.BlockSpec(rp["cw"].shape, const2),
            pl.BlockSpec(rp["cb"].shape, const2),
            pl.BlockSpec(rp["wg"].shape, const3),
            pl.BlockSpec(rp["ba"].shape, const2),
            pl.BlockSpec(rp["bx"].shape, const2),
            pl.BlockSpec(rp["lam"].shape, const2),
            pl.BlockSpec(h0.shape, const2),
        ],
        out_specs=[
            pl.BlockSpec((tm, din), lambda i: (nt - 1 - i, 0)),
            pl.BlockSpec((tm, dr), lambda i: (nt - 1 - i, 0)),
            pl.BlockSpec(h0.shape, const2),
        ],
        out_shape=[
            jax.ShapeDtypeStruct((n, din), F32),
            jax.ShapeDtypeStruct((n, dr), F32),
            jax.ShapeDtypeStruct(h0.shape, F32),
        ],
        scratch_shapes=[
            pltpu.VMEM((tm, d), BF16),
            pltpu.VMEM((halo, dr), F32),
            pltpu.VMEM((3 * SUBLANES, dr), F32),
            pltpu.VMEM(h0.shape, F32),
        ],
        compiler_params=pltpu.CompilerParams(
            dimension_semantics=("arbitrary",),
            vmem_limit_bytes=56 * 1024 * 1024),
        name="proj_bwd_scan",
    )(_as_view(x, view), sc, sh, g, w, rp["cw"], rp["cb"], rp["wg"], rp["ba"], rp["bx"], rp["lam"], h0)


def _rows(ref, lo, n, batch_major=False):
    if len(ref.shape) == 2:
        return ref[lo:lo + n, :]
    if batch_major:
        r = ref.shape[0]
        return jnp.concatenate([ref[:, t, :] for t in range(lo // r, (lo + n) // r)], axis=0)
    r = ref.shape[1]
    return ref[lo // r:(lo + n) // r, :, :].reshape(n, ref.shape[2])


def _set_rows(ref, lo, n, val, batch_major=False):
    if len(ref.shape) == 2:
        ref[lo:lo + n, :] = val
    elif batch_major:
        r = ref.shape[0]
        for k, t in enumerate(range(lo // r, (lo + n) // r)):
            ref[:, t, :] = val[k * r:(k + 1) * r, :]
    else:
        r = ref.shape[1]
        ref[lo // r:(lo + n) // r, :, :] = val.reshape(n // r, r, ref.shape[2])


def _rglru_unit(pre_r, pre_i, xc, ba, bx, c8, h):
    gate_r = _sigmoid(pre_r + ba)
    gate_i = _sigmoid(pre_i + bx)
    a = jnp.exp(c8 * gate_r)
    s = 1.0 - a * a
    v = jnp.where(s > 0.0, s * lax.rsqrt(s), 0.0) * (gate_i * xc)
    return a * h + v


def _mix_body(bg_ref, cg_ref, xa_ref, xr_ref, gr_ref, cgh_ref, xah_ref, hb_ref, x_ref,
              cw_ref, cb_ref, wg_ref, ba_ref, bx_ref, lam_ref, h0_ref,
              caw_ref, wout_ref, gn_ref, gate_ref,
              o_ref, hl_ref,
              xcar_ref, zcar_ref, par_ref, hc_ref, y_ref, *, r, mp, x_batch_major):
    m, dr = xr_ref.shape
    dc = bg_ref.shape[1]
    nb = r // SUBLANES
    ng = dr // LANES
    nblk, gw, _ = wg_ref.shape
    halo = (RNN_CONV_W - 1) * r
    npieces = m // mp
    i = pl.program_id(0)
    last = pl.num_programs(0) - 1

    @pl.when(i == 0)
    def _init():
        xcar_ref[...] = jnp.zeros((halo, dr), F32)
        zcar_ref[...] = jnp.zeros((r, dc), F32)
        hc_ref[...] = h0_ref[...]
        _fill_rnn_params(par_ref, ba_ref, bx_ref, lam_ref)

    f32 = lambda ref, lo, n: ref[lo:lo + n, :].astype(F32)
    gate_rows = jnp.concatenate([gate_ref[...]] * (mp // MOD_ROWS), axis=0)
    h = list(_load_state(hc_ref, nb))
    for p in range(npieces):
        lo = p * mp
        if p == 0:
            xe = jnp.concatenate([xcar_ref[...], f32(xr_ref, 0, mp)], axis=0)
        else:
            xe = f32(xr_ref, lo - halo, mp + halo)
        xc = cb_ref[...] + cw_ref[0:1, :] * xe[0:mp, :]
        for k in range(1, RNN_CONV_W):
            xc = xc + cw_ref[k:k + 1, :] * xe[k * r:k * r + mp, :]
        pre = [jnp.dot(xc[:, k * gw:(k + 1) * gw].astype(BF16), wg_ref[k], preferred_element_type=F32)
               for k in range(nblk)]
        for t in range(mp // r):
            for sb in range(nb):
                lr = t * r + sb * SUBLANES
                rs = slice(lo + lr, lo + lr + SUBLANES)
                for g in range(ng):
                    cs = slice(g * LANES, (g + 1) * LANES)
                    k, off = divmod(g * LANES, gw)
                    hh = _rglru_unit(pre[k][lr:lr + SUBLANES, off:off + LANES],
                                     pre[k][lr:lr + SUBLANES, gw + off:gw + off + LANES],
                                     xc[lr:lr + SUBLANES, cs],
                                     par_ref[0:8, cs], par_ref[8:16, cs], par_ref[16:24, cs],
                                     h[sb * ng + g])
                    h[sb * ng + g] = hh
                    y_ref[rs, dc + g * LANES:dc + (g + 1) * LANES] = (
                        (hh + hb_ref[rs, cs]) * jax.nn.gelu(gr_ref[rs, cs].astype(F32)))

        z_prev = zcar_ref[...] if p == 0 else f32(cg_ref, lo - r, r) * f32(xa_ref, lo - r, r)
        if p == npieces - 1:
            z_next = jnp.where(i < last, f32(cgh_ref, 0, r) * f32(xah_ref, 0, r), 0.0)
        else:
            z_next = f32(cg_ref, lo + mp, r) * f32(xa_ref, lo + mp, r)
        ze = jnp.concatenate([z_prev, f32(cg_ref, lo, mp) * f32(xa_ref, lo, mp), z_next], axis=0)
        conv = (caw_ref[0:1, :] * ze[0:mp, :] + caw_ref[1:2, :] * ze[r:r + mp, :]
                + caw_ref[2:3, :] * ze[2 * r:2 * r + mp, :])
        y_ref[lo:lo + mp, 0:dc] = f32(bg_ref, lo, mp) * conv

        yo = jnp.dot(y_ref[lo:lo + mp, :].astype(BF16), wout_ref[...], preferred_element_type=F32)
        _set_rows(o_ref, lo, mp, _rows(x_ref, lo, mp, x_batch_major) + gate_rows * _rms(yo, gn_ref[...]))

    xcar_ref[...] = f32(xr_ref, m - halo, halo)
    zcar_ref[...] = f32(cg_ref, m - r, r) * f32(xa_ref, m - r, r)
    _store_state(hc_ref, h, nb)
    hl_ref[...] = hc_ref[...]


def _mix(proj, hb, x, view, rp, h0, caw, wout, li, gn, gate, r, m, mp, cols):
    n, d = proj.shape[0], view["d"]
    dr = h0.shape[1]
    dc = caw.shape[1]
    dmix = wout.shape[-2]
    nt = n // m
    hr = r
    nh = n // hr
    const2 = lambda i: (0, 0)
    const3 = lambda i: (0, 0, 0)
    halo = (RNN_CONV_W - 1) * r
    col = lambda k: pl.BlockSpec((m, dc), lambda i: (i, k))
    nxt = lambda k: pl.BlockSpec((hr, dc), lambda i: (jnp.minimum((i + 1) * (m // hr), nh - 1), k))
    xspec = _x_spec(view, m)
    out_view = dict(kind="flat", d=d) if view["kind"] == "batch" else view
    out_struct = jax.ShapeDtypeStruct((n, d) if view["kind"] == "batch" else _as_view(x, view).shape, F32)
    return pl.pallas_call(
        functools.partial(_mix_body, r=r, mp=mp, x_batch_major=view["kind"] == "batch"),
        grid=(nt,),
        in_specs=[
            col(cols["bg"]), col(cols["cg"]), col(cols["xa"]), col(cols["xr"]), col(cols["gr"]),
            nxt(cols["cg"]), nxt(cols["xa"]),
            pl.BlockSpec((m, dr), lambda i: (i, 0)),
            xspec,
            pl.BlockSpec(rp["cw"].shape, const2),
            pl.BlockSpec(rp["cb"].shape, const2),
            pl.BlockSpec(rp["wg"].shape, const3),
            pl.BlockSpec(rp["ba"].shape, const2),
            pl.BlockSpec(rp["bx"].shape, const2),
            pl.BlockSpec(rp["lam"].shape, const2),
            pl.BlockSpec(h0.shape, const2),
            pl.BlockSpec(caw.shape, const2),
            pl.BlockSpec((None,) + wout.shape[1:], lambda i: (li, 0, 0), pipeline_mode=pl.Buffered(1)),
            pl.BlockSpec(gn.shape, const2),
            pl.BlockSpec(gate.shape, const2),
        ],
        out_specs=[
            _x_spec(out_view, m),
            pl.BlockSpec(h0.shape, const2),
        ],
        out_shape=[
            out_struct,
            jax.ShapeDtypeStruct(h0.shape, F32),
        ],
        scratch_shapes=[
            pltpu.VMEM((halo, dr), F32),
            pltpu.VMEM((r, dc), F32),
            pltpu.VMEM((3 * SUBLANES, dr), F32),
            pltpu.VMEM(h0.shape, F32),
            pltpu.VMEM((m, dmix), F32),
        ],
        compiler_params=pltpu.CompilerParams(
            dimension_semantics=("arbitrary",),
            vmem_limit_bytes=56 * 1024 * 1024),
        name="mix_fwd",
    )(proj, proj, proj, proj, proj, proj, proj, hb, _as_view(x, view),
      rp["cw"], rp["cb"], rp["wg"], rp["ba"], rp["bx"], rp["lam"], h0, caw, wout, gn, gate)


def _ffn_body(x_ref, xp_ref, xn_ref, sc_ref, sh_ref, gate_ref, gin_ref, gout_ref,
              wg_ref, wv_ref, cg_ref, cv_ref, wd_ref, o_ref, u_ref, acc_ref, *, r, ts, piece, out_batch_major):
    tm, d = x_ref.shape
    tf = wd_ref.shape[0]
    i = pl.program_id(0)
    j = pl.program_id(1)
    ni = pl.num_programs(0)
    nj = pl.num_programs(1)
    hal = HALO_ROWS
    rows_all = tm + 2 * r

    def conv(h, c_ref, cols):
        cw = c_ref[:, cols]
        h_prev = jnp.concatenate([h[tm:tm + r, :], h[0:tm - r, :]], axis=0)
        h_next = jnp.concatenate([h[r:tm, :], h[tm + r:tm + 2 * r, :]], axis=0)
        return cw[0:1, :] * h_prev + cw[1:2, :] * h[0:tm, :] + cw[2:3, :] * h_next

    def sub_chunk(k, hg, hv):
        cols = slice(ts * k, ts * (k + 1))
        a = (jax.nn.gelu(conv(hg, cg_ref, cols)) * conv(hv, cv_ref, cols)).astype(BF16)
        return jnp.dot(a, wd_ref[cols, :], preferred_element_type=F32)

    def up(u, k):
        cols = slice(ts * k, ts * (k + 1))
        return (jnp.dot(u, wg_ref[:, cols], preferred_element_type=F32),
                jnp.dot(u, wv_ref[:, cols], preferred_element_type=F32))

    def chunk(first):
        u = u_ref[...]
        acc = first
        for k in range(0 if first is None else 1, tf // ts):
            p = sub_chunk(k, *up(u, k))
            acc = p if acc is None else acc + p
        return acc

    @pl.when(j == 0)
    def _first_chunk():
        def modnorm(xq):
            n = xq.shape[0]
            return _rms(xq, gin_ref[...]) * (1.0 + sc_ref[0:n, :]) + sh_ref[0:n, :]

        u_halo = modnorm(jnp.concatenate([xp_ref[hal - r:hal, :], xn_ref[0:r, :]], axis=0))
        u_halo = jnp.concatenate([jnp.where(i > 0, u_halo[0:r, :], 0.0),
                                  jnp.where(i < ni - 1, u_halo[r:2 * r, :], 0.0)], axis=0).astype(BF16)
        hg, hv = [], []
        for lo in range(0, tm, piece):
            n = min(piece, tm - lo)
            parts = [modnorm(x_ref[q:q + hal, :]).astype(BF16) for q in range(lo, lo + n, hal)]
            if lo + n == tm:
                parts.append(u_halo)
                n += 2 * r
            u_p = jnp.concatenate(parts, axis=0)
            u_ref[lo:lo + n, :] = u_p
            g_p, v_p = up(u_p, 0)
            hg.append(g_p)
            hv.append(v_p)
        first = sub_chunk(0, jnp.concatenate(hg, axis=0), jnp.concatenate(hv, axis=0))
        acc_ref[...] = chunk(first)

    @pl.when(jnp.logical_and(j > 0, j < nj - 1))
    def _middle_chunk():
        acc_ref[...] += chunk(None)

    @pl.when(jnp.logical_and(j > 0, j == nj - 1))
    def _last_chunk():
        total = acc_ref[...] + chunk(None)
        gate = jnp.concatenate([gate_ref[...]] * (piece // MOD_ROWS), axis=0)
        for lo in range(0, tm, piece):
            rows = slice(lo, lo + piece)
            _set_rows(o_ref, lo, piece, x_ref[rows, :] + gate * _rms(total[rows, :], gout_ref[...]),
                      out_batch_major)


def _ffn(x, sc, sh, gate, gin, gout, wup, cw, wd, li, r, tm, tf, ts, piece, out_batch_major):
    n, d = x.shape
    dff = wd.shape[-2]
    assert n % tm == 0 and dff % tf == 0 and tf % ts == 0 and tm % HALO_ROWS == 0
    assert dff // tf >= 2, "the first and last hidden chunk of a tile must be different grid steps"
    ni = n // tm
    nj = dff // tf
    nh = n // HALO_ROWS
    const = lambda i, j: (0, 0)
    return pl.pallas_call(
        functools.partial(_ffn_body, r=r, ts=ts, piece=piece, out_batch_major=out_batch_major),
        grid=(ni, nj),
        in_specs=[
            pl.BlockSpec((tm, d), lambda i, j: (i, 0)),
            pl.BlockSpec((HALO_ROWS, d), lambda i, j: (jnp.maximum(i * (tm // HALO_ROWS) - 1, 0), 0)),
            pl.BlockSpec((HALO_ROWS, d), lambda i, j: (jnp.minimum((i + 1) * (tm // HALO_ROWS), nh - 1), 0)),
            pl.BlockSpec(sc.shape, const),
            pl.BlockSpec(sh.shape, const),
            pl.BlockSpec(gate.shape, const),
            pl.BlockSpec(gin.shape, const),
            pl.BlockSpec(gout.shape, const),
            pl.BlockSpec((None, d, tf), lambda i, j: (li, 0, j)),
            pl.BlockSpec((None, d, tf), lambda i, j: (li, 0, nj + j)),
            pl.BlockSpec((FFN_CONV_W, tf), lambda i, j: (0, j)),
            pl.BlockSpec((FFN_CONV_W, tf), lambda i, j: (0, nj + j)),
            pl.BlockSpec((None, tf, d), lambda i, j: (li, j, 0)),
        ],
        out_specs=(pl.BlockSpec((r, tm // r, d), lambda i, j: (0, i, 0)) if out_batch_major
                   else pl.BlockSpec((tm, d), lambda i, j: (i, 0))),
        out_shape=jax.ShapeDtypeStruct((r, n // r, d) if out_batch_major else (n, d), F32),
        scratch_shapes=[
            pltpu.VMEM((tm + 2 * r, d), BF16),
            pltpu.VMEM((tm, d), F32),
        ],
        compiler_params=pltpu.CompilerParams(
            dimension_semantics=("parallel", "arbitrary"),
            vmem_limit_bytes=56 * 1024 * 1024),
        name="conv_ffn",
    )(x, x, x, sc, sh, gate, gin, gout, wup, wup, cw, cw, wd)


def _block_diag(w, gw):
    heads, hd, _ = w.shape
    hpb = gw // hd
    w = w.reshape(heads // hpb, hpb, hd, hd)
    eye = jnp.eye(hpb, dtype=w.dtype)
    return jnp.einsum("nhij,hg->nhigj", w, eye).reshape(heads // hpb, gw, gw)


def _plan(d, dr, dff, n_rows, r, grid_rows):
    seq_rows = grid_rows * r if grid_rows else n_rows
    return dict(
        proj_rows=min(256, seq_rows),
        mix_rows=min(256, seq_rows),
        ffn_rows=min(512, n_rows),
        ffn_cols=min(1024, dff),
        norm_piece=min(128, n_rows),
        mix_piece=min(128, seq_rows),
        ffn_sub=min(2 * MXU_WIDTH, dff),
        proj_cols=3 * (d - dr),
    )


def _tile_rows(mod, r, rows):
    return jnp.tile(mod, (rows // r, 1))


def _layer(x, r, grid_rows, mods, h0, lw, plan, first, last):
    d = x.shape[-1]
    n = x.size // d
    sh1, sc1, g1, sh2, sc2, g2 = [_tile_rows(m, r, MOD_ROWS) for m in jnp.split(mods, N_MOD, axis=-1)]
    if first:
        assert not grid_rows
        view = dict(kind="batch", r=r, d=d)
    elif grid_rows:
        view = dict(kind="grid", r=r, d=d, grid_rows=grid_rows)
    else:
        view = dict(kind="flat", d=d)
    ng = lw["norm_g"]
    dr = h0.shape[-1]
    proj, hb, last_b = _projscan(x, view, sc1, sh1, ng[0:1], lw["w_in"], lw["li"], lw["rnn"][1], h0[:, 1], r,
                                 plan["proj_rows"], plan["mix_piece"], plan["proj_cols"], lw["cols"]["xr"] * dr)
    x, last_f = _mix(proj, hb, x, view, lw["rnn"][0], h0[:, 0], lw["conv_a_w"], lw["w_out"], lw["li"],
                     ng[1:2], g1, r, plan["mix_rows"], plan["mix_piece"], lw["cols"])
    x = x.reshape(n, d)
    x = _ffn(x, sc2, sh2, g2, ng[2:3], ng[3:4], lw["ffn_up"], lw["ffn_conv_w"], lw["ffn_down"], lw["li"],
             r, plan["ffn_rows"], plan["ffn_cols"], plan["ffn_sub"], plan["norm_piece"], last)
    return x, jnp.stack([last_f, last_b], axis=1)


def _forward(x_prompt, x_sample, state_h, c, c_ctx, w_ada, b_ada, norm_g, w_in, conv_a_w,
             rnn_conv_w, rnn_conv_b, rnn_w_a, rnn_b_a, rnn_w_x, rnn_b_x, rnn_lam, w_out,
             ffn_up, ffn_conv_w, ffn_down, plan_overrides=None):
    b, s, d = x_prompt.shape
    bd, l, _ = x_sample.shape
    depth = w_ada.shape[0]
    dr = rnn_lam.shape[-1]
    dc = conv_a_w.shape[-1]
    dff = ffn_down.shape[1]
    grid_rows = l // GRID_W
    gw = min(MXU_WIDTH, dr)

    plan_p = _plan(d, dr, dff, s * b, b, 0)
    plan_s = _plan(d, dr, dff, l * bd, bd, grid_rows)
    for p in (plan_p, plan_s):
        p.update(plan_overrides or {})

    xp, xs = x_prompt, x_sample

    cc = jnp.concatenate([c, jnp.broadcast_to(c_ctx[None, :], (SUBLANES, d))], axis=0)
    mods = _modulation(cc, w_ada, b_ada, min(1024, N_MOD * d))

    h_zero = jnp.zeros((b, 2, dr), F32)
    w_in_b, w_out_b, ffn_up_b, ffn_down_b = (w.astype(BF16) for w in (w_in, w_out, ffn_up, ffn_down))
    states = []
    for li in range(depth):
        rnn = []
        for di in range(2):
            wg = jnp.concatenate([_block_diag(rnn_w_a[li, di], gw), _block_diag(rnn_w_x[li, di], gw)],
                                 axis=-1).astype(BF16)
            rnn.append(dict(cw=rnn_conv_w[li, di], cb=rnn_conv_b[li, di][None, :], wg=wg,
                            ba=rnn_b_a[li, di][None, :], bx=rnn_b_x[li, di][None, :],
                            lam=rnn_lam[li, di][None, :], xr_col=3 * dc // dr))
        lw = dict(
            norm_g=norm_g[li],
            li=li,
            w_in=w_in_b,
            conv_a_w=conv_a_w[li],
            rnn=rnn,
            w_out=w_out_b,
            ffn_up=ffn_up_b,
            ffn_conv_w=ffn_conv_w[li],
            ffn_down=ffn_down_b,
            cols=dict(bg=0, cg=1, xa=2, xr=3 * dc // dr, gr=3 * dc // dr + 1),
        )
        mod_ctx = jnp.broadcast_to(mods[li, bd:bd + 1], (b, N_MOD * d))
        first, last = li == 0, li == depth - 1
        xp, st = _layer(xp, b, 0, mod_ctx, h_zero, lw, plan_p, first, last)
        states.append(st)
        xs, _ = _layer(xs, bd, grid_rows if li % 2 == 1 else 0, mods[li, :bd], state_h[:, li], lw, plan_s,
                       first, last)

    return xp, xs, jnp.stack(states, axis=1)


def kernel(x_prompt, x_sample, state_h, c, c_ctx, w_ada, b_ada, norm_g, w_in, conv_a_w, rnn_conv_w,
           rnn_conv_b, rnn_w_a, rnn_b_a, rnn_w_x, rnn_b_x, rnn_lam, w_out, ffn_up, ffn_conv_w, ffn_down):
    return _forward(x_prompt, x_sample, state_h, c, c_ctx, w_ada, b_ada, norm_g, w_in, conv_a_w,
                    rnn_conv_w, rnn_conv_b, rnn_w_a, rnn_b_a, rnn_w_x, rnn_b_x, rnn_lam, w_out,
                    ffn_up, ffn_conv_w, ffn_down)
```
